```python
import math
import jax, jax.numpy as jnp
from jax import lax
import numpy as np

D_MODEL = 1024
BATCH = 4
SEQ = 4096
DEPTH = 2
DEC_BATCH = 32
DEC_SEQ = 1
PAST_LEN = 8192
PAGE_SIZE = 128

N_MIXERS = 2
N_META = 16
EXPAND = 2
D_INNER = EXPAND * D_MODEL
D_CONV = D_INNER
CONV_W = 3
N_HEADS = 16
HEAD_QK = 64
HEAD_V = 2 * HEAD_QK
ROT_DIM = HEAD_QK // 4
ROPE_THETA = 500000.0
Q_BLOCK = 128
EPS = 1e-6
N_A = (DEPTH + 1) // 2
N_B = DEPTH // 2
QK_COLS = N_HEADS * 2 * HEAD_QK
V_COLS = N_HEADS * HEAD_V

kernel_name = "metatoken_shortconv_diffattn_hybrid_step"


def rms_norm(x, g):
    xf = x.astype(jnp.float32)
    xf = xf * lax.rsqrt(jnp.mean(xf * xf, axis=-1, keepdims=True) + EPS)
    return (xf * g.astype(jnp.float32)).astype(x.dtype)


def rope_partial(x, pos):
    half = ROT_DIM // 2
    inv = ROPE_THETA ** (-jnp.arange(half, dtype=jnp.float32) * 2.0 / ROT_DIM)
    ang = pos.astype(jnp.float32)[:, None] * inv[None, :]
    cos = jnp.cos(ang)[:, None, None, :]
    sin = jnp.sin(ang)[:, None, None, :]
    xr = x[..., :ROT_DIM].astype(jnp.float32)
    x1, x2 = xr[..., :half], xr[..., half:]
    rot = jnp.concatenate([x1 * cos - x2 * sin, x2 * cos + x1 * sin], axis=-1).astype(x.dtype)
    return jnp.concatenate([rot, x[..., ROT_DIM:]], axis=-1)


def short_conv_mixer(xn, conv_prev, w_in, conv_w, w_out):
    proj = xn @ w_in
    c, b, v, z = jnp.split(proj, 4, axis=-1)
    u = c * v
    u_ext = jnp.concatenate([conv_prev.astype(u.dtype), u], axis=1)
    T = xn.shape[1]
    conv = sum(conv_w[t] * u_ext[:, t:t + T] for t in range(CONV_W))
    y = b * conv * jax.nn.silu(z)
    return y @ w_out, u_ext[:, -(CONV_W - 1):]


def lambda_full(lq1, lk1, lq2, lk2, lam_init):
    f = lambda a: a.astype(jnp.float32)
    return jnp.exp(jnp.sum(f(lq1) * f(lk1))) - jnp.exp(jnp.sum(f(lq2) * f(lk2))) + lam_init


def diff_attn_project(xn, pos, w_in, q_norm, k_norm):
    N, T, _ = xn.shape
    proj = xn @ w_in
    q, k, v, z = jnp.split(proj, [QK_COLS, 2 * QK_COLS, 2 * QK_COLS + V_COLS], axis=-1)
    q = rope_partial(rms_norm(q.reshape(N, T, N_HEADS, 2, HEAD_QK), q_norm), pos)
    k = rope_partial(rms_norm(k.reshape(N, T, N_HEADS, 2, HEAD_QK), k_norm), pos)
    v = v.reshape(N, T, N_HEADS, HEAD_V)
    return q, k, v, z


def diff_scores_combine(q, k, v, q_pos, k_pos, lam):
    s = jnp.einsum('nqhcd,nkhcd->nhcqk', q, k).astype(jnp.float32) * (HEAD_QK ** -0.5)
    mask = k_pos[None, :] <= q_pos[:, None]
    s = jnp.where(mask, s, -jnp.inf)
    p = jax.nn.softmax(s, axis=-1)
    a = p[:, :, 0] - lam * p[:, :, 1]
    return jnp.einsum('nhqk,nkhe->nqhe', a.astype(v.dtype), v)


def prompt_diff_attention(q, k, v, lam):
    N, L = q.shape[0], q.shape[1]
    Lp = -(-L // Q_BLOCK) * Q_BLOCK
    padq = ((0, 0), (0, Lp - L), (0, 0), (0, 0), (0, 0))
    qp, kp = jnp.pad(q, padq), jnp.pad(k, padq)
    vp = jnp.pad(v, ((0, 0), (0, Lp - L), (0, 0), (0, 0)))
    pos = jnp.arange(Lp)
    nb = Lp // Q_BLOCK
    qb = qp.reshape(N, nb, Q_BLOCK, N_HEADS, 2, HEAD_QK).transpose(1, 0, 2, 3, 4, 5)
    qpos = pos.reshape(nb, Q_BLOCK)
    out = lax.map(lambda a: diff_scores_combine(a[0], kp, vp, a[1], pos, lam), (qb, qpos))
    out = out.transpose(1, 0, 2, 3, 4).reshape(N, Lp, N_HEADS, HEAD_V)
    return out[:, :L]


def diff_attn_output(o, z, subln, w_out, lam_init):
    o = rms_norm(o, subln) * (1.0 - lam_init)
    N, T = o.shape[0], o.shape[1]
    o = o.reshape(N, T, V_COLS) * jax.nn.silu(z)
    return o @ w_out


def gather_pages(cache, page_table):
    g = cache[page_table]
    return g.reshape(g.shape[0], g.shape[1] * g.shape[2], g.shape[3], g.shape[4])


def setup_inputs(seed: int = 0) -> dict:
    key = jax.random.key(seed)
    ks = jax.random.split(key, 24)
    n_pages = PAST_LEN // PAGE_SIZE
    n_used = DEC_BATCH * n_pages
    n_phys = n_used + max(1, n_used // 4)
    perm = jax.random.permutation(ks[0], n_phys)
    page_table = perm[:n_used].reshape(DEC_BATCH, n_pages).astype(jnp.int32)
    nrm = lambda k, s, sc: jax.random.normal(k, s, jnp.float32) * sc
    gain = lambda k, s: 1.0 + 0.05 * jax.random.normal(k, s, jnp.float32)
    return {
        "x_prompt": nrm(ks[1], (BATCH, SEQ, D_MODEL), 1.0),
        "x_sample": nrm(ks[2], (DEC_BATCH, DEC_SEQ, D_MODEL), 1.0),
        "state_conv": nrm(ks[3], (N_A, DEC_BATCH, CONV_W - 1, D_CONV), 0.5),
        "cache_k": nrm(ks[4], (N_B, n_phys, PAGE_SIZE, N_HEADS, 2 * HEAD_QK), 1.0),
        "cache_v": nrm(ks[5], (N_B, n_phys, PAGE_SIZE, N_HEADS, HEAD_V), 1.0),
        "page_table": page_table,
        "meta_tokens": nrm(ks[6], (N_META, D_MODEL), 1.0),
        "norm_a": gain(ks[7], (N_A, D_MODEL)),
        "w_in_a": nrm(ks[8], (N_A, D_MODEL, 4 * D_CONV), D_MODEL ** -0.5),
        "conv_w_a": nrm(ks[9], (N_A, CONV_W, D_CONV), CONV_W ** -0.5),
        "w_out_a": nrm(ks[10], (N_A, D_CONV, D_MODEL), D_CONV ** -0.5),
        "norm_b": gain(ks[11], (N_B, D_MODEL)),
        "w_in_b": nrm(ks[12], (N_B, D_MODEL, 2 * QK_COLS + 2 * V_COLS), D_MODEL ** -0.5),
        "q_norm_b": gain(ks[13], (N_B, HEAD_QK)),
        "k_norm_b": gain(ks[14], (N_B, HEAD_QK)),
        "lambda_q1_b": nrm(ks[15], (N_B, HEAD_QK), 0.1),
        "lambda_k1_b": nrm(ks[16], (N_B, HEAD_QK), 0.1),
        "lambda_q2_b": nrm(ks[17], (N_B, HEAD_QK), 0.1),
        "lambda_k2_b": nrm(ks[18], (N_B, HEAD_QK), 0.1),
        "subln_b": gain(ks[19], (N_B, HEAD_V)),
        "w_out_b": nrm(ks[20], (N_B, V_COLS, D_MODEL), V_COLS ** -0.5),
    }


def reference(x_prompt, x_sample, state_conv, cache_k, cache_v, page_table, meta_tokens,
              norm_a, w_in_a, conv_w_a, w_out_a, norm_b, w_in_b, q_norm_b, k_norm_b,
              lambda_q1_b, lambda_k1_b, lambda_q2_b, lambda_k2_b, subln_b, w_out_b):
    B = x_prompt.shape[0]
    DB, Ts = x_sample.shape[0], x_sample.shape[1]
    meta = jnp.broadcast_to(meta_tokens[None].astype(x_prompt.dtype), (B, N_META, D_MODEL))
    hp = jnp.concatenate([meta, x_prompt], axis=1)
    hs = x_sample
    L = hp.shape[1]
    pos_p = jnp.arange(L)
    pos_s = PAST_LEN + jnp.arange(Ts)
    pos_all = jnp.arange(PAST_LEN + Ts)
    conv_p_l, conv_s_l, kp_l, vp_l, ks_l, vs_l = [], [], [], [], [], []
    for i in range(DEPTH):
        j = i // N_MIXERS
        if i % N_MIXERS == 0:
            xn_p = rms_norm(hp, norm_a[j])
            xn_s = rms_norm(hs, norm_a[j])
            zeros = jnp.zeros((B, CONV_W - 1, D_CONV), hp.dtype)
            yp, sp = short_conv_mixer(xn_p, zeros, w_in_a[j], conv_w_a[j], w_out_a[j])
            ys, ss = short_conv_mixer(xn_s, state_conv[j], w_in_a[j], conv_w_a[j], w_out_a[j])
            conv_p_l.append(sp)
            conv_s_l.append(ss)
        else:
            lam_init = 0.8 - 0.6 * math.exp(-0.3 * i)
            lam = lambda_full(lambda_q1_b[j], lambda_k1_b[j], lambda_q2_b[j], lambda_k2_b[j], lam_init)
            xn_p = rms_norm(hp, norm_b[j])
            xn_s = rms_norm(hs, norm_b[j])
            qp, kp, vp, zp = diff_attn_project(xn_p, pos_p, w_in_b[j], q_norm_b[j], k_norm_b[j])
            op = prompt_diff_attention(qp, kp, vp, lam)
            yp = diff_attn_output(op, zp, subln_b[j], w_out_b[j], lam_init)
            qs, kss, vss, zs = diff_attn_project(xn_s, pos_s, w_in_b[j], q_norm_b[j], k_norm_b[j])
            k_past = gather_pages(cache_k[j], page_table).astype(kss.dtype)
            v_past = gather_pages(cache_v[j], page_table).astype(vss.dtype)
            k_all = jnp.concatenate([k_past.reshape(DB, PAST_LEN, N_HEADS, 2, HEAD_QK), kss], axis=1)
            v_all = jnp.concatenate([v_past, vss], axis=1)
            os_ = diff_scores_combine(qs, k_all, v_all, pos_s, pos_all, lam)
            ys = diff_attn_output(os_, zs, subln_b[j], w_out_b[j], lam_init)
            kp_l.append(kp.reshape(B, L, N_HEADS, 2 * HEAD_QK))
            vp_l.append(vp)
            ks_l.append(kss.reshape(DB, Ts, N_HEADS, 2 * HEAD_QK))
            vs_l.append(vss)
        hp = hp + yp
        hs = hs + ys
    y_prompt = hp[:, N_META:]
    return (y_prompt, hs, jnp.stack(conv_p_l), jnp.stack(conv_s_l), jnp.stack(kp_l), jnp.stack(vp_l), jnp.stack(ks_l), jnp.stack(vs_l))
```

```python
import functools
import math

import numpy as np
import jax
import jax.numpy as jnp
from jax import lax
from jax.experimental import pallas as pl
from jax.experimental.pallas import tpu as pltpu

D_MODEL = 1024
D_CONV = 2048
CONV_W = 3
N_HEADS = 16
HEAD_QK = 64
HEAD_V = 128
ROT_DIM = 16
ROPE_THETA = 500000.0
EPS = 1e-6
N_META = 16
QK_COLS = N_HEADS * 2 * HEAD_QK
V_COLS = N_HEADS * HEAD_V
N_MAPS = 2 * N_HEADS
LAM_INIT = 0.8 - 0.6 * math.exp(-0.3 * 1)
SCORE_SCALE = HEAD_QK ** -0.5
LOG2_HEAD_QK = HEAD_QK.bit_length() - 1
LOG2_HEAD_V = HEAD_V.bit_length() - 1
LOG2_N_HEADS = N_HEADS.bit_length() - 1

V7X_LANES = 128
V7X_SUBLANES = 8
V7X_MXU_DIM = 256
V7X_VMEM_BYTES = 64 * 1024 * 1024

F32 = jnp.float32
BF16 = jnp.bfloat16


def _dot(a, b):
    return jnp.dot(a, b, preferred_element_type=F32)


def _dot_nt(a, b):
    return lax.dot_general(a, b, (((1,), (1,)), ((), ())), preferred_element_type=F32)


def _rms_rows(x, gain):
    ms = jnp.mean(x * x, axis=-1, keepdims=True)
    return x * lax.rsqrt(ms + EPS) * gain


def _silu(z):
    return z * jax.nn.sigmoid(z)


def _lambda_full(lq1_ref, lk1_ref, lq2_ref, lk2_ref):
    a = jnp.sum(lq1_ref[...] * lk1_ref[...], axis=-1, keepdims=True)
    b = jnp.sum(lq2_ref[...] * lk2_ref[...], axis=-1, keepdims=True)
    return jnp.exp(a) - jnp.exp(b) + LAM_INIT


def _params(semantics, vmem_bytes):
    return pltpu.CompilerParams(dimension_semantics=semantics,
                                vmem_limit_bytes=min(int(vmem_bytes), V7X_VMEM_BYTES - (4 << 20)))


def _resident(shape):
    return pl.BlockSpec(shape, lambda *_: (0,) * len(shape), pipeline_mode=pl.Buffered(1))


def _conv_gate_out(xn, u, um1, um2, wb, wz, cw, wout):
    conv = cw[0:1, :] * um2 + cw[1:2, :] * um1 + cw[2:3, :] * u
    y = _dot(xn, wb) * conv * _silu(_dot(xn, wz))
    return _dot(y.astype(BF16), wout)


def _mixer_a_body_kernel(x_ref, g_ref, win_ref, cw_ref, wout_ref, init_ref,
                         hp_ref, st_ref, carry_ref, *, tm, tc):
    @pl.when(pl.program_id(1) == 0)
    def _():
        carry_ref[...] = init_ref[...]

    x = x_ref[...]
    xn = _rms_rows(x, g_ref[...]).astype(BF16)
    row = lax.broadcasted_iota(jnp.int32, (tm, tc), 0)
    acc = jnp.zeros((tm, D_MODEL), F32)
    for j in range(D_CONV // tc):
        lo = j * tc
        u = _dot(xn, win_ref[:, lo:lo + tc]) * _dot(xn, win_ref[:, 2 * D_CONV + lo:2 * D_CONV + lo + tc])
        prev = carry_ref[:, lo:lo + tc]
        p1 = prev[7:8, :]
        p2 = prev[6:7, :]
        um1 = jnp.where(row == 0, p1, pltpu.roll(u, 1, 0))
        um2 = jnp.where(row == 0, p2, jnp.where(row == 1, p1, pltpu.roll(u, 2, 0)))
        carry_ref[:, lo:lo + tc] = u[tm - V7X_SUBLANES:tm, :]
        acc = acc + _conv_gate_out(
            xn, u, um1, um2,
            win_ref[:, D_CONV + lo:D_CONV + lo + tc],
            win_ref[:, 3 * D_CONV + lo:3 * D_CONV + lo + tc],
            cw_ref[:, lo:lo + tc], wout_ref[lo:lo + tc, :])
    hp_ref[...] = x + acc
    st_ref[0] = carry_ref[...]


def _mixer_a_small_kernel(x_ref, g_ref, wc_ref, wb_ref, wv_ref, wz_ref, cw_ref, wout_ref,
                          p1_ref, p2_ref, hp_ref, u_ref, *, n_meta):
    x = x_ref[...]
    xn = _rms_rows(x, g_ref[...]).astype(BF16)
    u = _dot(xn, wc_ref[...]) * _dot(xn, wv_ref[...])
    row = lax.broadcasted_iota(jnp.int32, u.shape, 0)
    um1 = jnp.where(row == 0, 0.0, jnp.where(row < n_meta, pltpu.roll(u, 1, 0), p1_ref[...]))
    um2 = jnp.where(row < 2, 0.0, jnp.where(row < n_meta, pltpu.roll(u, 2, 0), p2_ref[...]))
    u_ref[...] = u
    contrib = _conv_gate_out(xn, u, um1, um2, wb_ref[...], wz_ref[...], cw_ref[...], wout_ref[...])

    @pl.when(pl.program_id(0) == 0)
    def _():
        hp_ref[...] = x + contrib

    @pl.when(pl.program_id(0) != 0)
    def _():
        hp_ref[...] += contrib


def _mixer_a_body(x2d, gain, w_in, conv_w, w_out, init8, *, batch, seq, tm=512, tc=512):
    nt = seq // tm
    vmem = (4 * tm * D_MODEL * 4
            + w_in.size * 2 + w_out.size * 2
            + 10 * tm * tc * 4 + 2 * tm * D_MODEL * 4
            + (8 << 20))
    return pl.pallas_call(
        functools.partial(_mixer_a_body_kernel, tm=tm, tc=tc),
        grid=(batch, nt),
        in_specs=[
            pl.BlockSpec((tm, D_MODEL), lambda b, t: (b * nt + t, 0)),
            _resident((1, D_MODEL)),
            _resident(w_in.shape),
            _resident(conv_w.shape),
            _resident(w_out.shape),
            _resident(init8.shape),
        ],
        out_specs=[
            pl.BlockSpec((tm, D_MODEL), lambda b, t: (b * nt + t, 0)),
            pl.BlockSpec((1, V7X_SUBLANES, D_CONV), lambda b, t: (b, 0, 0)),
        ],
        out_shape=[
            jax.ShapeDtypeStruct((batch * seq, D_MODEL), F32),
            jax.ShapeDtypeStruct((batch, V7X_SUBLANES, D_CONV), F32),
        ],
        scratch_shapes=[pltpu.VMEM((V7X_SUBLANES, D_CONV), F32)],
        compiler_params=_params(("arbitrary", "arbitrary"), vmem),
        name="mixer_a_body",
    )(x2d, gain, w_in, conv_w, w_out, init8)


def _mixer_a_small(x, gain, w_in, conv_w, w_out, p1, p2, *, n_meta, tc=512):
    rows = x.shape[0]
    nj = D_CONV // tc
    w_spec = lambda k: pl.BlockSpec((D_MODEL, tc), lambda j, k=k: (0, k * nj + j))
    vmem = 2 * (4 * D_MODEL * tc * 2 + tc * D_MODEL * 2) + (8 << 20)
    return pl.pallas_call(
        functools.partial(_mixer_a_small_kernel, n_meta=n_meta),
        grid=(nj,),
        in_specs=[
            pl.BlockSpec((rows, D_MODEL), lambda j: (0, 0)),
            pl.BlockSpec((1, D_MODEL), lambda j: (0, 0)),
            w_spec(0), w_spec(1), w_spec(2), w_spec(3),
            pl.BlockSpec((CONV_W, tc), lambda j: (0, j)),
            pl.BlockSpec((tc, D_MODEL), lambda j: (j, 0)),
            pl.BlockSpec((rows, tc), lambda j: (0, j)),
            pl.BlockSpec((rows, tc), lambda j: (0, j)),
        ],
        out_specs=[
            pl.BlockSpec((rows, D_MODEL), lambda j: (0, 0)),
            pl.BlockSpec((rows, tc), lambda j: (0, j)),
        ],
        out_shape=[
            jax.ShapeDtypeStruct((rows, D_MODEL), F32),
            jax.ShapeDtypeStruct((rows, D_CONV), F32),
        ],
        compiler_params=_params(("arbitrary",), vmem),
        name="mixer_a_small",
    )(x, gain, w_in, w_in, w_in, w_in, conv_w, w_out, p1, p2)


def _rope_tables(pos, tc):
    lane = lax.broadcasted_iota(jnp.int32, (1, V7X_LANES), 1)
    l64 = lane & (HEAD_QK - 1)
    half = ROT_DIM // 2
    idx = jnp.where(l64 < half, l64, l64 - half).astype(F32)
    inv = jnp.where(l64 < ROT_DIM, jnp.exp(idx * (-2.0 / ROT_DIM * math.log(ROPE_THETA))), 0.0)
    ang = pos * inv
    cos = jnp.cos(ang)
    sin = jnp.sin(ang)
    lo = jnp.where(l64 < half, -sin, 0.0)
    hi = jnp.where((l64 >= half) & (l64 < ROT_DIM), sin, 0.0)
    reps = tc // V7X_LANES
    return tuple(jnp.concatenate([t] * reps, axis=1) for t in (cos, lo, hi))


def _norm_rope(p, gain, bd, tables):
    tc = p.shape[1]
    sq = (p * p).astype(BF16)
    ss = jnp.concatenate(
        [_dot(sq[:, i * V7X_MXU_DIM:(i + 1) * V7X_MXU_DIM], bd) for i in range(tc // V7X_MXU_DIM)],
        axis=1)
    pn = p * lax.rsqrt(ss * (1.0 / HEAD_QK) + EPS) * gain
    cos, lo, hi = tables
    half = ROT_DIM // 2
    return pn * cos + pltpu.roll(pn, tc - half, 1) * lo + pltpu.roll(pn, half, 1) * hi


def _proj_b_kernel(h_ref, pos_ref, g_ref, win_ref, qn_ref, kn_ref, bd_ref, *out_refs,
                   tc, head_major):
    xn = _rms_rows(h_ref[...], g_ref[...]).astype(BF16)
    tables = _rope_tables(pos_ref[...], tc)
    bd = bd_ref[...]
    heads_per_chunk = tc // HEAD_V
    if head_major:
        q_ref, kf_ref, vf_ref, zs_ref, kb_ref, vb_ref = out_refs
    else:
        q_ref, kf_ref, vf_ref, zs_ref = out_refs
    for j in range(QK_COLS // tc):
        lo = j * tc
        q = _norm_rope(_dot(xn, win_ref[:, lo:lo + tc]), qn_ref[:, lo:lo + tc], bd, tables)
        q = q * SCORE_SCALE
        k = _norm_rope(_dot(xn, win_ref[:, QK_COLS + lo:QK_COLS + lo + tc]),
                       kn_ref[:, lo:lo + tc], bd, tables)
        v = _dot(xn, win_ref[:, 2 * QK_COLS + lo:2 * QK_COLS + lo + tc])
        z = _dot(xn, win_ref[:, 2 * QK_COLS + V_COLS + lo:2 * QK_COLS + V_COLS + lo + tc])
        kf_ref[:, lo:lo + tc] = k
        vf_ref[:, lo:lo + tc] = v
        zs_ref[:, lo:lo + tc] = _silu(z).astype(zs_ref.dtype)
        if head_major:
            for hh in range(heads_per_chunk):
                h = j * heads_per_chunk + hh
                sl = slice(hh * HEAD_V, (hh + 1) * HEAD_V)
                q_ref[0, h] = q[:, sl].astype(BF16)
                kb_ref[0, h] = k[:, sl].astype(BF16)
                vb_ref[0, h] = v[:, sl].astype(BF16)
        else:
            q_ref[:, lo:lo + tc] = q


def _proj_b(h2d, pos, gain, w_in, qn, kn, bd, *, batch, seq, tm, head_major, tc=512):
    nt = seq // tm
    rows = batch * seq
    row_spec = lambda cols: pl.BlockSpec((tm, cols), lambda b, t: (b * nt + t, 0))
    hm_spec = pl.BlockSpec((1, N_HEADS, tm, HEAD_V), lambda b, t: (b, 0, t, 0))
    hm_shape = jax.ShapeDtypeStruct((batch, N_HEADS, seq, HEAD_V), BF16)
    row_shape = lambda dt: jax.ShapeDtypeStruct((rows, QK_COLS), dt)
    if head_major:
        out_specs = [hm_spec, row_spec(QK_COLS), row_spec(V_COLS), row_spec(V_COLS), hm_spec, hm_spec]
        out_shape = [hm_shape, row_shape(F32), row_shape(F32), row_shape(BF16), hm_shape, hm_shape]
    else:
        out_specs = [row_spec(QK_COLS), row_spec(QK_COLS), row_spec(V_COLS), row_spec(V_COLS)]
        out_shape = [row_shape(F32), row_shape(F32), row_shape(F32), row_shape(F32)]
    vmem = (2 * tm * D_MODEL * 4 + w_in.size * 2
            + 2 * tm * QK_COLS * (4 + 4 + 4 + 4 + 4)
            + 12 * tm * tc * 4 + (8 << 20))
    return pl.pallas_call(
        functools.partial(_proj_b_kernel, tc=tc, head_major=head_major),
        grid=(batch, nt),
        in_specs=[
            row_spec(D_MODEL),
            pl.BlockSpec((tm, V7X_LANES), lambda b, t: (t, 0)),
            _resident((1, D_MODEL)),
            _resident(w_in.shape),
            _resident((1, QK_COLS)),
            _resident((1, QK_COLS)),
            _resident(bd.shape),
        ],
        out_specs=out_specs,
        out_shape=out_shape,
        compiler_params=_params(("arbitrary", "arbitrary"), vmem),
        name="proj_b_body" if head_major else "proj_b_small",
    )(h2d, pos, gain, w_in, qn, kn, bd)


def _attn_body_kernel(q_ref, k_ref, v_ref, kp_ref, vp_ref, zs_ref, sub_ref,
                      lq1_ref, lk1_ref, lq2_ref, lk2_ref, g_ref, *, tq):
    i = pl.program_id(2)
    q = q_ref[0, 0]
    lane = lax.broadcasted_iota(jnp.int32, q.shape, 1)
    zero = jnp.zeros_like(q)
    qm = (jnp.where(lane < HEAD_QK, q, zero), jnp.where(lane >= HEAD_QK, q, zero))

    def online(state, s, vb):
        m, l, acc = state
        m_new = jnp.maximum(m, jnp.max(s, axis=1, keepdims=True))
        alpha = jnp.exp(m - m_new)
        p = jnp.exp(s - m_new)
        return (m_new, alpha * l + jnp.sum(p, axis=1, keepdims=True),
                alpha * acc + _dot(p.astype(BF16), vb))

    kp = kp_ref[0]
    vp = vp_ref[0]
    states = []
    for c in range(2):
        s = _dot_nt(qm[c], kp)
        m = jnp.max(s, axis=1, keepdims=True)
        p = jnp.exp(s - m)
        states.append((m, jnp.sum(p, axis=1, keepdims=True), _dot(p.astype(BF16), vp)))

    def full_block(j, states):
        off = pl.multiple_of(j * tq, tq)
        kb = k_ref[0, 0, pl.ds(off, tq), :]
        vb = v_ref[0, 0, pl.ds(off, tq), :]
        return tuple(online(states[c], _dot_nt(qm[c], kb), vb) for c in range(2))

    states = lax.fori_loop(0, i, full_block, tuple(states))

    off = pl.multiple_of(i * tq, tq)
    kb = k_ref[0, 0, pl.ds(off, tq), :]
    vb = v_ref[0, 0, pl.ds(off, tq), :]
    causal = (lax.broadcasted_iota(jnp.int32, (tq, tq), 1)
              <= lax.broadcasted_iota(jnp.int32, (tq, tq), 0))
    states = tuple(
        online(states[c], jnp.where(causal, _dot_nt(qm[c], kb), -jnp.inf), vb) for c in range(2))

    lam = _lambda_full(lq1_ref, lk1_ref, lq2_ref, lk2_ref)
    (_, l0, a0), (_, l1, a1) = states
    o = a0 * (1.0 / l0) - lam * (a1 * (1.0 / l1))
    on = _rms_rows(o, sub_ref[...]) * (1.0 - LAM_INIT)
    g_ref[...] = (on * zs_ref[...].astype(F32)).astype(g_ref.dtype)


def _attn_body(q, kb, vb, kp, vp, zs, subln, lams, *, batch, seq, tq=512):
    nq = seq // tq
    head_spec = pl.BlockSpec((1, 1, seq, HEAD_V), lambda b, h, i: (b, h, 0, 0))
    pre_spec = pl.BlockSpec((1, N_META, HEAD_V), lambda b, h, i: (h, 0, 0))
    vec = pl.BlockSpec((1, HEAD_QK), lambda b, h, i: (0, 0))
    vmem = 4 * seq * HEAD_V * 2 + 8 * tq * tq * 4 + (8 << 20)
    return pl.pallas_call(
        functools.partial(_attn_body_kernel, tq=tq),
        grid=(batch, N_HEADS, nq),
        in_specs=[
            pl.BlockSpec((1, 1, tq, HEAD_V), lambda b, h, i: (b, h, i, 0)),
            head_spec, head_spec, pre_spec, pre_spec,
            pl.BlockSpec((tq, HEAD_V), lambda b, h, i: (b * nq + i, h)),
            pl.BlockSpec((1, HEAD_V), lambda b, h, i: (0, 0)),
            vec, vec, vec, vec,
        ],
        out_specs=pl.BlockSpec((tq, HEAD_V), lambda b, h, i: (b * nq + i, h)),
        out_shape=jax.ShapeDtypeStruct((batch * seq, V_COLS), BF16),
        compiler_params=_params(("arbitrary", "arbitrary", "arbitrary"), vmem),
        name="attn_body",
    )(q, kb, vb, kp, vp, zs, subln, *lams)


def _half_sums(x):
    lane = lax.broadcasted_iota(jnp.int32, x.shape, 1)
    low = lane < HEAD_QK
    s_lo = jnp.sum(jnp.where(low, x, 0.0), axis=1, keepdims=True)
    s_hi = jnp.sum(jnp.where(low, 0.0, x), axis=1, keepdims=True)
    return jnp.where(low, s_lo, s_hi)


def _attn_decode_kernel(pt_ref, q_ref, kc_ref, vc_ref, kn_ref, vn_ref, zs_ref, sub_ref, hs_ref,
                        lq1_ref, lk1_ref, lq2_ref, lk2_ref, g_ref,
                        m_ref, l_ref, acc_ref, accx_ref, *, page):
    del pt_ref
    p_idx = pl.program_id(1)

    @pl.when(p_idx == 0)
    def _():
        m_ref[...] = jnp.full(m_ref.shape, -jnp.inf, F32)
        l_ref[...] = jnp.zeros(l_ref.shape, F32)
        acc_ref[...] = jnp.zeros(acc_ref.shape, F32)
        accx_ref[...] = jnp.zeros(accx_ref.shape, F32)

    q = q_ref[0]
    rows = page * N_HEADS
    prod = (kc_ref[0, 0] * q[None]).astype(BF16).reshape(rows, HEAD_V)
    s = _dot(prod, hs_ref[...])
    s3 = s.reshape(page, N_HEADS, HEAD_V)
    m_old = m_ref[...]
    m_new = jnp.maximum(m_old, jnp.max(s3, axis=0))
    alpha = jnp.exp(m_old - m_new)
    p3 = jnp.exp(s3 - m_new[None])
    px3 = pltpu.roll(p3.reshape(rows, HEAD_V), HEAD_QK, 1).reshape(page, N_HEADS, HEAD_V)
    v3 = vc_ref[0, 0]
    l_ref[...] = alpha * l_ref[...] + jnp.sum(p3, axis=0)
    acc_ref[...] = alpha * acc_ref[...] + jnp.sum(v3 * p3, axis=0)
    accx_ref[...] = pltpu.roll(alpha, HEAD_QK, 1) * accx_ref[...] + jnp.sum(v3 * px3, axis=0)
    m_ref[...] = m_new

    @pl.when(p_idx == pl.num_programs(1) - 1)
    def _():
        vn = vn_ref[0]
        s_self = _half_sums(q * kn_ref[0])
        m_old = m_ref[...]
        m_fin = jnp.maximum(m_old, s_self)
        a_fin = jnp.exp(m_old - m_fin)
        p_self = jnp.exp(s_self - m_fin)
        inv_l = 1.0 / (a_fin * l_ref[...] + p_self)
        w = (a_fin * acc_ref[...] + p_self * vn) * inv_l
        a_x = pltpu.roll(a_fin, HEAD_QK, 1)
        p_x = pltpu.roll(p_self, HEAD_QK, 1)
        wx = (a_x * accx_ref[...] + p_x * vn) * pltpu.roll(inv_l, HEAD_QK, 1)
        low = lax.broadcasted_iota(jnp.int32, w.shape, 1) < HEAD_QK
        lam = _lambda_full(lq1_ref, lk1_ref, lq2_ref, lk2_ref)
        d = jnp.where(low, w, wx) - lam * jnp.where(low, wx, w)
        on = _rms_rows(d, sub_ref[...]) * (1.0 - LAM_INIT)
        g_ref[0] = on * zs_ref[0]


def _attn_decode(page_table, q_s, cache_k, cache_v, k_new, v_new, zs, subln, hs, lams):
    dec_batch, n_pages = page_table.shape
    page = cache_k.shape[2]
    per_b = pl.BlockSpec((1, N_HEADS, HEAD_V), lambda b, p, pt: (b, 0, 0))
    page_spec = pl.BlockSpec((1, 1, page, N_HEADS, HEAD_V), lambda b, p, pt: (0, pt[b, p], 0, 0, 0))
    vec = pl.BlockSpec((1, HEAD_QK), lambda b, p, pt: (0, 0))
    acc = pltpu.VMEM((N_HEADS, HEAD_V), F32)
    page_bytes = page * N_HEADS * HEAD_V * 4
    vmem = 4 * page_bytes + 8 * page_bytes + (8 << 20)
    return pl.pallas_call(
        functools.partial(_attn_decode_kernel, page=page),
        grid_spec=pltpu.PrefetchScalarGridSpec(
            num_scalar_prefetch=1,
            grid=(dec_batch, n_pages),
            in_specs=[
                per_b, page_spec, page_spec, per_b, per_b, per_b,
                pl.BlockSpec((1, HEAD_V), lambda b, p, pt: (0, 0)),
                pl.BlockSpec((HEAD_V, HEAD_V), lambda b, p, pt: (0, 0)),
                vec, vec, vec, vec,
            ],
            out_specs=per_b,
            scratch_shapes=[acc, acc, acc, acc],
        ),
        out_shape=jax.ShapeDtypeStruct((dec_batch, N_HEADS, HEAD_V), F32),
        compiler_params=_params(("arbitrary", "arbitrary"), vmem),
        name="attn_decode",
    )(page_table, q_s, cache_k, cache_v, k_new, v_new, zs, subln, hs, *lams)


def _out_b_kernel(g_ref, h_ref, w_ref, y_ref):
    y_ref[...] = h_ref[...] + _dot(g_ref[...].astype(BF16), w_ref[...])


def _out_b(g, h, w_out, *, tm):
    rows = g.shape[0]
    vmem = 2 * tm * (V_COLS * 4 + 2 * D_MODEL * 4) + w_out.size * 2 + 2 * tm * D_MODEL * 4 + (8 << 20)
    return pl.pallas_call(
        _out_b_kernel,
        grid=(rows // tm,),
        in_specs=[
            pl.BlockSpec((tm, V_COLS), lambda t: (t, 0)),
            pl.BlockSpec((tm, D_MODEL), lambda t: (t, 0)),
            _resident(w_out.shape),
        ],
        out_specs=pl.BlockSpec((tm, D_MODEL), lambda t: (t, 0)),
        out_shape=jax.ShapeDtypeStruct((rows, D_MODEL), F32),
        compiler_params=_params(("arbitrary",), vmem),
        name="out_b",
    )(g, h, w_out)


def kernel(x_prompt, x_sample, state_conv, cache_k, cache_v, page_table, meta_tokens, norm_a, w_in_a, conv_w_a, w_out_a, norm_b, w_in_b, q_norm_b, k_norm_b, lambda_q1_b, lambda_k1_b, lambda_q2_b, lambda_k2_b, subln_b, w_out_b):
    batch, seq, _ = x_prompt.shape
    dec_batch, dec_seq, _ = x_sample.shape
    n_meta = meta_tokens.shape[0]
    assert norm_a.shape[0] == 1 and norm_b.shape[0] == 1 and dec_seq == 1 and n_meta == N_META
    n_pages = page_table.shape[1]
    page = cache_k.shape[2]
    past_len = n_pages * page

    w_in_a_bf = w_in_a[0].astype(BF16)
    w_out_a_bf = w_out_a[0].astype(BF16)
    w_in_b_bf = w_in_b[0].astype(BF16)
    w_out_b_bf = w_out_b[0].astype(BF16)
    gain_a = norm_a[0][None, :]
    gain_b = norm_b[0][None, :]

    x_small = jnp.concatenate([meta_tokens, x_sample[:, 0, :]], axis=0)
    zpad = jnp.zeros((n_meta, D_CONV), F32)
    p1 = jnp.concatenate([zpad, state_conv[0, :, 1, :]], axis=0)
    p2 = jnp.concatenate([zpad, state_conv[0, :, 0, :]], axis=0)
    h_small, u_small = _mixer_a_small(x_small, gain_a, w_in_a_bf, conv_w_a[0], w_out_a_bf,
                                      p1, p2, n_meta=n_meta)
    init8 = u_small[n_meta - V7X_SUBLANES:n_meta]
    h_body, st_body = _mixer_a_body(x_prompt.reshape(batch * seq, D_MODEL), gain_a, w_in_a_bf,
                                    conv_w_a[0], w_out_a_bf, init8, batch=batch, seq=seq)
    conv_prompt = st_body[:, V7X_SUBLANES - (CONV_W - 1):, :][None]
    conv_sample = jnp.stack([state_conv[0, :, 1, :], u_small[n_meta:]], axis=1)[None]

    qn_t = jnp.tile(q_norm_b[0], N_MAPS)[None, :]
    kn_t = jnp.tile(k_norm_b[0], N_MAPS)[None, :]
    gi = np.arange(V7X_MXU_DIM) // HEAD_QK
    bd = jnp.asarray(gi[:, None] == gi[None, :], dtype=BF16)
    lanes1 = jnp.ones((1, V7X_LANES), F32)
    pos_body = (n_meta + jnp.arange(seq, dtype=F32))[:, None] * lanes1
    pos_small = jnp.concatenate([jnp.arange(n_meta, dtype=F32),
                                 jnp.full((dec_batch,), float(past_len), F32)])[:, None] * lanes1
    q_s, k_small, v_small, zs_small = _proj_b(
        h_small, pos_small, gain_b, w_in_b_bf, qn_t, kn_t, bd,
        batch=1, seq=n_meta + dec_batch, tm=n_meta + dec_batch, head_major=False)
    q_hm, k_body, v_body, zs_body, k_hm, v_hm = _proj_b(
        h_body, pos_body, gain_b, w_in_b_bf, qn_t, kn_t, bd,
        batch=batch, seq=seq, tm=512, head_major=True)

    lams = tuple(a[0][None, :] for a in (lambda_q1_b, lambda_k1_b, lambda_q2_b, lambda_k2_b))

    to_heads = lambda a: a[:n_meta].reshape(n_meta, N_HEADS, HEAD_V).transpose(1, 0, 2).astype(BF16)
    g_body = _attn_body(q_hm, k_hm, v_hm, to_heads(k_small), to_heads(v_small), zs_body,
                        subln_b[0][None, :], lams, batch=batch, seq=seq)
    y_body = _out_b(g_body, h_body, w_out_b_bf, tm=512)

    per_head = lambda a: a[n_meta:].reshape(dec_batch, N_HEADS, HEAD_V)
    li = np.arange(HEAD_V) // HEAD_QK
    half_sum = jnp.asarray(li[:, None] == li[None, :], dtype=BF16)
    g_dec = _attn_decode(page_table, per_head(q_s), cache_k, cache_v, per_head(k_small),
                         per_head(v_small), per_head(zs_small), subln_b[0][None, :], half_sum, lams)
    y_dec = _out_b(g_dec.reshape(dec_batch, V_COLS), h_small[n_meta:], w_out_b_bf, tm=dec_batch)

    def with_meta(small, body):
        meta = jnp.broadcast_to(small[:n_meta].reshape(1, n_meta, N_HEADS, HEAD_V),
                                (batch, n_meta, N_HEADS, HEAD_V))
        return jnp.concatenate([meta, body.reshape(batch, seq, N_HEADS, HEAD_V)], axis=1)[None]

    return (
        y_body.reshape(batch, seq, D_MODEL),
        y_dec.reshape(dec_batch, 1, D_MODEL),
        conv_prompt,
        conv_sample,
        with_meta(k_small, k_body),
        with_meta(v_small, v_body),
        k_small[n_meta:].reshape(1, dec_batch, 1, N_HEADS, HEAD_V),
        v_small[n_meta:].reshape(1, dec_batch, 1, N_HEADS, HEAD_V),
    )
```

```python
import functools
import math

import numpy as np
import jax
import jax.numpy as jnp
from jax import lax
from jax.experimental import pallas as pl
from jax.experimental.pallas import tpu as pltpu

D_MODEL = 1024
D_CONV = 2048
CONV_W = 3
N_HEADS = 16
HEAD_QK = 64
HEAD_V = 128
ROT_DIM = 16
ROPE_THETA = 500000.0
EPS = 1e-6
N_META = 16
QK_COLS = N_HEADS * 2 * HEAD_QK
V_COLS = N_HEADS * HEAD_V
N_MAPS = 2 * N_HEADS
LAM_INIT = 0.8 - 0.6 * math.exp(-0.3 * 1)
SCORE_SCALE = HEAD_QK ** -0.5
Q_SCALE = SCORE_SCALE * math.log2(math.e)
LOG2_HEAD_QK = HEAD_QK.bit_length() - 1
LOG2_HEAD_V = HEAD_V.bit_length() - 1
LOG2_N_HEADS = N_HEADS.bit_length() - 1

V7X_LANES = 128
V7X_SUBLANES = 8
V7X_MXU_DIM = 256
V7X_VMEM_BYTES = 64 * 1024 * 1024

ATTN_TILE = 512
DECODE_CHUNK = 16

F32 = jnp.float32
BF16 = jnp.bfloat16


def _dot(a, b):
    return jnp.dot(a, b, preferred_element_type=F32)


def _rms_rows(x, gain):
    ms = jnp.mean(x * x, axis=-1, keepdims=True)
    return x * lax.rsqrt(ms + EPS) * gain


def _silu(z):
    return z * jax.nn.sigmoid(z)


def _lambda_full(lq1_ref, lk1_ref, lq2_ref, lk2_ref):
    a = jnp.sum(lq1_ref[...] * lk1_ref[...], axis=-1, keepdims=True)
    b = jnp.sum(lq2_ref[...] * lk2_ref[...], axis=-1, keepdims=True)
    return jnp.exp(a) - jnp.exp(b) + LAM_INIT


def _params(semantics, vmem_bytes):
    return pltpu.CompilerParams(dimension_semantics=semantics,
                                vmem_limit_bytes=min(int(vmem_bytes), V7X_VMEM_BYTES - (4 << 20)))


def _resident(shape):
    return pl.BlockSpec(shape, lambda *_: (0,) * len(shape), pipeline_mode=pl.Buffered(1))


def _conv_gate_out(xn, u, um1, um2, wb, wz, cw, wout):
    conv = cw[0:1, :] * um2 + cw[1:2, :] * um1 + cw[2:3, :] * u
    y = _dot(xn, wb) * conv * _silu(_dot(xn, wz))
    return _dot(y.astype(BF16), wout)


def _mixer_a_body_kernel(x_ref, g_ref, win_ref, cw_ref, wout_ref, init_ref,
                         hp_ref, st_ref, carry_ref, *, tm, tc):
    @pl.when(pl.program_id(1) == 0)
    def _():
        carry_ref[...] = init_ref[...]

    x = x_ref[...]
    xn = _rms_rows(x, g_ref[...]).astype(BF16)
    row = lax.broadcasted_iota(jnp.int32, (tm, tc), 0)
    acc = jnp.zeros((tm, D_MODEL), F32)
    for j in range(D_CONV // tc):
        lo = j * tc
        u = _dot(xn, win_ref[:, lo:lo + tc]) * _dot(xn, win_ref[:, 2 * D_CONV + lo:2 * D_CONV + lo + tc])
        prev = carry_ref[:, lo:lo + tc]
        p1 = prev[7:8, :]
        p2 = prev[6:7, :]
        um1 = jnp.where(row == 0, p1, pltpu.roll(u, 1, 0))
        um2 = jnp.where(row == 0, p2, jnp.where(row == 1, p1, pltpu.roll(u, 2, 0)))
        carry_ref[:, lo:lo + tc] = u[tm - V7X_SUBLANES:tm, :]
        acc = acc + _conv_gate_out(
            xn, u, um1, um2,
            win_ref[:, D_CONV + lo:D_CONV + lo + tc],
            win_ref[:, 3 * D_CONV + lo:3 * D_CONV + lo + tc],
            cw_ref[:, lo:lo + tc], wout_ref[lo:lo + tc, :])
    hp_ref[...] = x + acc
    st_ref[0] = carry_ref[...]


def _mixer_a_small_kernel(x_ref, g_ref, wc_ref, wb_ref, wv_ref, wz_ref, cw_ref, wout_ref,
                          p1_ref, p2_ref, hp_ref, u_ref, *, n_meta):
    x = x_ref[...]
    xn = _rms_rows(x, g_ref[...]).astype(BF16)
    u = _dot(xn, wc_ref[...]) * _dot(xn, wv_ref[...])
    row = lax.broadcasted_iota(jnp.int32, u.shape, 0)
    um1 = jnp.where(row == 0, 0.0, jnp.where(row < n_meta, pltpu.roll(u, 1, 0), p1_ref[...]))
    um2 = jnp.where(row < 2, 0.0, jnp.where(row < n_meta, pltpu.roll(u, 2, 0), p2_ref[...]))
    u_ref[...] = u
    contrib = _conv_gate_out(xn, u, um1, um2, wb_ref[...], wz_ref[...], cw_ref[...], wout_ref[...])

    @pl.when(pl.program_id(0) == 0)
    def _():
        hp_ref[...] = x + contrib

    @pl.when(pl.program_id(0) != 0)
    def _():
        hp_ref[...] += contrib


def _mixer_a_body(x2d, gain, w_in, conv_w, w_out, init8, *, batch, seq, tm=512, tc=512):
    nt = seq // tm
    vmem = (4 * tm * D_MODEL * 4
            + w_in.size * 2 + w_out.size * 2
            + 10 * tm * tc * 4 + 2 * tm * D_MODEL * 4
            + (8 << 20))
    return pl.pallas_call(
        functools.partial(_mixer_a_body_kernel, tm=tm, tc=tc),
        grid=(batch, nt),
        in_specs=[
            pl.BlockSpec((tm, D_MODEL), lambda b, t: (b * nt + t, 0)),
            _resident((1, D_MODEL)),
            _resident(w_in.shape),
            _resident(conv_w.shape),
            _resident(w_out.shape),
            _resident(init8.shape),
        ],
        out_specs=[
            pl.BlockSpec((tm, D_MODEL), lambda b, t: (b * nt + t, 0)),
            pl.BlockSpec((1, V7X_SUBLANES, D_CONV), lambda b, t: (b, 0, 0)),
        ],
        out_shape=[
            jax.ShapeDtypeStruct((batch * seq, D_MODEL), F32),
            jax.ShapeDtypeStruct((batch, V7X_SUBLANES, D_CONV), F32),
        ],
        scratch_shapes=[pltpu.VMEM((V7X_SUBLANES, D_CONV), F32)],
        compiler_params=_params(("arbitrary", "arbitrary"), vmem),
        name="mixer_a_body",
    )(x2d, gain, w_in, conv_w, w_out, init8)


def _mixer_a_small(x, gain, w_in, conv_w, w_out, p1, p2, *, n_meta, tc=512):
    rows = x.shape[0]
    nj = D_CONV // tc
    w_spec = lambda k: pl.BlockSpec((D_MODEL, tc), lambda j, k=k: (0, k * nj + j))
    vmem = 2 * (4 * D_MODEL * tc * 2 + tc * D_MODEL * 2) + (8 << 20)
    return pl.pallas_call(
        functools.partial(_mixer_a_small_kernel, n_meta=n_meta),
        grid=(nj,),
        in_specs=[
            pl.BlockSpec((rows, D_MODEL), lambda j: (0, 0)),
            pl.BlockSpec((1, D_MODEL), lambda j: (0, 0)),
            w_spec(0), w_spec(1), w_spec(2), w_spec(3),
            pl.BlockSpec((CONV_W, tc), lambda j: (0, j)),
            pl.BlockSpec((tc, D_MODEL), lambda j: (j, 0)),
            pl.BlockSpec((rows, tc), lambda j: (0, j)),
            pl.BlockSpec((rows, tc), lambda j: (0, j)),
        ],
        out_specs=[
            pl.BlockSpec((rows, D_MODEL), lambda j: (0, 0)),
            pl.BlockSpec((rows, tc), lambda j: (0, j)),
        ],
        out_shape=[
            jax.ShapeDtypeStruct((rows, D_MODEL), F32),
            jax.ShapeDtypeStruct((rows, D_CONV), F32),
        ],
        compiler_params=_params(("arbitrary",), vmem),
        name="mixer_a_small",
    )(x, gain, w_in, w_in, w_in, w_in, conv_w, w_out, p1, p2)


def _rope_tables(pos, tc):
    lane = lax.broadcasted_iota(jnp.int32, (1, V7X_LANES), 1)
    l64 = lane & (HEAD_QK - 1)
    half = ROT_DIM // 2
    idx = jnp.where(l64 < half, l64, l64 - half).astype(F32)
    inv = jnp.where(l64 < ROT_DIM, jnp.exp(idx * (-2.0 / ROT_DIM * math.log(ROPE_THETA))), 0.0)
    ang = pos * inv
    cos = jnp.cos(ang)
    sin = jnp.sin(ang)
    lo = jnp.where(l64 < half, -sin, 0.0)
    hi = jnp.where((l64 >= half) & (l64 < ROT_DIM), sin, 0.0)
    reps = tc // V7X_LANES
    return tuple(jnp.concatenate([t] * reps, axis=1) for t in (cos, lo, hi))


def _norm_rope(p, gain, bd, tables):
    tc = p.shape[1]
    sq = (p * p).astype(BF16)
    ss = jnp.concatenate(
        [_dot(sq[:, i * V7X_MXU_DIM:(i + 1) * V7X_MXU_DIM], bd) for i in range(tc // V7X_MXU_DIM)],
        axis=1)
    pn = p * lax.rsqrt(ss * (1.0 / HEAD_QK) + EPS) * gain
    cos, lo, hi = tables
    half = ROT_DIM // 2
    return pn * cos + pltpu.roll(pn, tc - half, 1) * lo + pltpu.roll(pn, half, 1) * hi


def _proj_b_kernel(h_ref, pos_ref, g_ref, win_ref, qn_ref, kn_ref, bd_ref, *out_refs,
                   tc, head_major):
    xn = _rms_rows(h_ref[...], g_ref[...]).astype(BF16)
    tables = _rope_tables(pos_ref[...], tc)
    bd = bd_ref[...]
    heads_per_chunk = tc // HEAD_V
    if head_major:
        q_ref, kf_ref, vf_ref, zs_ref, kb_ref, vt_ref = out_refs
    else:
        q_ref, kf_ref, vf_ref, zs_ref = out_refs
    for j in range(QK_COLS // tc):
        lo = j * tc
        q = _norm_rope(_dot(xn, win_ref[:, lo:lo + tc]), qn_ref[:, lo:lo + tc], bd, tables)
        q = q * Q_SCALE
        k = _norm_rope(_dot(xn, win_ref[:, QK_COLS + lo:QK_COLS + lo + tc]),
                       kn_ref[:, lo:lo + tc], bd, tables)
        v = _dot(xn, win_ref[:, 2 * QK_COLS + lo:2 * QK_COLS + lo + tc])
        z = _dot(xn, win_ref[:, 2 * QK_COLS + V_COLS + lo:2 * QK_COLS + V_COLS + lo + tc])
        kf_ref[:, lo:lo + tc] = k
        vf_ref[:, lo:lo + tc] = v
        zs_ref[:, lo:lo + tc] = _silu(z).astype(zs_ref.dtype)
        if head_major:
            for hh in range(heads_per_chunk):
                h = j * heads_per_chunk + hh
                sl = slice(hh * HEAD_V, (hh + 1) * HEAD_V)
                q_ref[0, h, 0] = q[:, sl].T.astype(BF16)
                kb_ref[0, h] = k[:, sl].astype(BF16)
                vt_ref[0, h, 0] = v[:, sl].T.astype(BF16)
        else:
            q_ref[:, lo:lo + tc] = q


def _proj_b(h2d, pos, gain, w_in, qn, kn, bd, *, batch, seq, tm, head_major, tc=512):
    nt = seq // tm
    rows = batch * seq
    row_spec = lambda cols: pl.BlockSpec((tm, cols), lambda b, t: (b * nt + t, 0))
    hm_spec = pl.BlockSpec((1, N_HEADS, tm, HEAD_V), lambda b, t: (b, 0, t, 0))
    hm_shape = jax.ShapeDtypeStruct((batch, N_HEADS, seq, HEAD_V), BF16)
    tr_spec = pl.BlockSpec((1, N_HEADS, 1, HEAD_V, tm), lambda b, t: (b, 0, t, 0, 0))
    tr_shape = jax.ShapeDtypeStruct((batch, N_HEADS, nt, HEAD_V, tm), BF16)
    row_shape = lambda dt: jax.ShapeDtypeStruct((rows, QK_COLS), dt)
    if head_major:
        out_specs = [tr_spec, row_spec(QK_COLS), row_spec(V_COLS), row_spec(V_COLS), hm_spec, tr_spec]
        out_shape = [tr_shape, row_shape(F32), row_shape(F32), row_shape(BF16), hm_shape, tr_shape]
    else:
        out_specs = [row_spec(QK_COLS), row_spec(QK_COLS), row_spec(V_COLS), row_spec(V_COLS)]
        out_shape = [row_shape(F32), row_shape(F32), row_shape(F32), row_shape(F32)]
    vmem = (2 * tm * D_MODEL * 4 + w_in.size * 2
            + 2 * tm * QK_COLS * (4 + 4 + 4 + 4 + 4)
            + 12 * tm * tc * 4 + (8 << 20))
    return pl.pallas_call(
        functools.partial(_proj_b_kernel, tc=tc, head_major=head_major),
        grid=(batch, nt),
        in_specs=[
            row_spec(D_MODEL),
            pl.BlockSpec((tm, V7X_LANES), lambda b, t: (t, 0)),
            _resident((1, D_MODEL)),
            _resident(w_in.shape),
            _resident((1, QK_COLS)),
            _resident((1, QK_COLS)),
            _resident(bd.shape),
        ],
        out_specs=out_specs,
        out_shape=out_shape,
        compiler_params=_params(("arbitrary", "arbitrary"), vmem),
        name="proj_b_body" if head_major else "proj_b_small",
    )(h2d, pos, gain, w_in, qn, kn, bd)


def _attn_body_kernel(qt_ref, k_ref, vt_ref, kp_ref, vpt_ref, zs_ref, sub_ref,
                      lq1_ref, lk1_ref, lq2_ref, lk2_ref, g_ref, *, tq):
    i = pl.program_id(2)
    n_odd = i & 1
    qt = qt_ref[0, 0, 0]
    drow = lax.broadcasted_iota(jnp.int32, qt.shape, 0)
    zero = jnp.zeros_like(qt)
    qtm = (jnp.where(drow < HEAD_QK, qt, zero), jnp.where(drow >= HEAD_QK, qt, zero))

    def online(state, st, vt):
        m, l, acc = state
        m_new = jnp.maximum(m, jnp.max(st, axis=0, keepdims=True))
        alpha = jnp.exp2(m - m_new)
        pt = jnp.exp2(st - m_new)
        return (m_new, alpha * l + jnp.sum(pt, axis=0, keepdims=True),
                alpha * acc + _dot(vt, pt.astype(BF16)))

    empty = (jnp.full((1, tq), -jnp.inf, F32), jnp.zeros((1, tq), F32), jnp.zeros((HEAD_V, tq), F32))
    states = (empty, empty)

    def scores(j):
        kb = k_ref[0, 0, pl.ds(pl.multiple_of(j * tq, tq), tq), :]
        return tuple(_dot(kb, qtm[c]) for c in range(2))

    def one_block(j, states):
        sts = scores(j)
        vt = vt_ref[0, 0, j]
        return tuple(online(states[c], sts[c], vt) for c in range(2))

    def two_blocks(p, states):
        ja = n_odd + 2 * p
        sa = scores(ja)
        sb = scores(ja + 1)
        va = vt_ref[0, 0, ja]
        vb = vt_ref[0, 0, ja + 1]
        states = tuple(online(states[c], sa[c], va) for c in range(2))
        return tuple(online(states[c], sb[c], vb) for c in range(2))

    states = lax.fori_loop(0, n_odd, one_block, states)
    states = lax.fori_loop(0, lax.shift_right_logical(i, 1), two_blocks, states)

    causal = (lax.broadcasted_iota(jnp.int32, (tq, tq), 0)
              <= lax.broadcasted_iota(jnp.int32, (tq, tq), 1))
    sts = scores(i)
    kp = kp_ref[0]
    sps = tuple(_dot(kp, qtm[c]) for c in range(2))
    vt = vt_ref[0, 0, i]
    vpt = vpt_ref[0]
    states = tuple(
        online(states[c], jnp.where(causal, sts[c], -jnp.inf), vt) for c in range(2))
    states = tuple(online(states[c], sps[c], vpt) for c in range(2))

    lam = _lambda_full(lq1_ref, lk1_ref, lq2_ref, lk2_ref)
    (_, l0, a0), (_, l1, a1) = states
    o = (a0 * (1.0 / l0) - lam * (a1 * (1.0 / l1))).T
    on = _rms_rows(o, sub_ref[...]) * (1.0 - LAM_INIT)
    g_ref[...] = (on * zs_ref[...].astype(F32)).astype(g_ref.dtype)


def _attn_body(qt, kb, vt, kp, vpt, zs, subln, lams, *, batch, seq, tq):
    nq = seq // tq
    tr_head = pl.BlockSpec((1, 1, nq, HEAD_V, tq), lambda b, h, i: (b, h, 0, 0, 0))
    vec = pl.BlockSpec((1, HEAD_QK), lambda b, h, i: (0, 0))
    vmem = 4 * seq * HEAD_V * 2 + 10 * tq * tq * 4 + (8 << 20)
    return pl.pallas_call(
        functools.partial(_attn_body_kernel, tq=tq),
        grid=(batch, N_HEADS, nq),
        in_specs=[
            pl.BlockSpec((1, 1, 1, HEAD_V, tq), lambda b, h, i: (b, h, i, 0, 0)),
            pl.BlockSpec((1, 1, seq, HEAD_V), lambda b, h, i: (b, h, 0, 0)),
            tr_head,
            pl.BlockSpec((1, N_META, HEAD_V), lambda b, h, i: (h, 0, 0)),
            pl.BlockSpec((1, HEAD_V, N_META), lambda b, h, i: (h, 0, 0)),
            pl.BlockSpec((tq, HEAD_V), lambda b, h, i: (b * nq + i, h)),
            pl.BlockSpec((1, HEAD_V), lambda b, h, i: (0, 0)),
            vec, vec, vec, vec,
        ],
        out_specs=pl.BlockSpec((tq, HEAD_V), lambda b, h, i: (b * nq + i, h)),
        out_shape=jax.ShapeDtypeStruct((batch * seq, V_COLS), BF16),
        compiler_params=_params(("arbitrary", "arbitrary", "arbitrary"), vmem),
        name="attn_body",
    )(qt, kb, vt, kp, vpt, zs, subln, *lams)


def _half_sums(x):
    lane = lax.broadcasted_iota(jnp.int32, x.shape, 1)
    low = lane < HEAD_QK
    s_lo = jnp.sum(jnp.where(low, x, 0.0), axis=1, keepdims=True)
    s_hi = jnp.sum(jnp.where(low, 0.0, x), axis=1, keepdims=True)
    return jnp.where(low, s_lo, s_hi)


def _attn_decode_kernel(pt_ref, q_ref, *refs, page, n_pg):
    del pt_ref
    k_refs = refs[:n_pg]
    v_refs = refs[n_pg:2 * n_pg]
    (kn_ref, vn_ref, zs_ref, sub_ref, hs_ref, lq1_ref, lk1_ref, lq2_ref, lk2_ref,
     g_ref, m_ref, l_ref, acc_ref, accx_ref) = refs[2 * n_pg:]
    p_idx = pl.program_id(1)

    @pl.when(p_idx == 0)
    def _():
        m_ref[...] = jnp.full(m_ref.shape, -jnp.inf, F32)
        l_ref[...] = jnp.zeros(l_ref.shape, F32)
        acc_ref[...] = jnp.zeros(acc_ref.shape, F32)
        accx_ref[...] = jnp.zeros(accx_ref.shape, F32)

    q = q_ref[0]
    hs = hs_ref[...]

    def page_scores(k_ref):
        prod = (k_ref[0, 0] * q[None]).astype(BF16).reshape(page * N_HEADS, HEAD_V)
        return _dot(prod, hs).reshape(page, N_HEADS, HEAD_V)

    m = m_ref[...]
    l = l_ref[...]
    acc = acc_ref[...]
    accx = accx_ref[...]
    s_cur = page_scores(k_refs[0])
    for i in range(n_pg):
        s_next = page_scores(k_refs[i + 1]) if i + 1 < n_pg else None
        m_new = jnp.maximum(m, jnp.max(s_cur, axis=0))
        alpha = jnp.exp2(m - m_new)
        l = alpha * l
        acc = alpha * acc
        accx = pltpu.roll(alpha, HEAD_QK, 1) * accx
        for t0 in range(0, page, DECODE_CHUNK):
            p3 = jnp.exp2(s_cur[t0:t0 + DECODE_CHUNK] - m_new[None])
            px3 = pltpu.roll(p3.reshape(DECODE_CHUNK * N_HEADS, HEAD_V), HEAD_QK, 1).reshape(p3.shape)
            v3 = v_refs[i][0, 0, t0:t0 + DECODE_CHUNK]
            l = l + jnp.sum(p3, axis=0)
            acc = acc + jnp.sum(v3 * p3, axis=0)
            accx = accx + jnp.sum(v3 * px3, axis=0)
        m = m_new
        s_cur = s_next
    m_ref[...] = m
    l_ref[...] = l
    acc_ref[...] = acc
    accx_ref[...] = accx

    @pl.when(p_idx == pl.num_programs(1) - 1)
    def _():
        vn = vn_ref[0]
        s_self = _half_sums(q * kn_ref[0])
        m_old = m_ref[...]
        m_fin = jnp.maximum(m_old, s_self)
        a_fin = jnp.exp2(m_old - m_fin)
        p_self = jnp.exp2(s_self - m_fin)
        inv_l = 1.0 / (a_fin * l_ref[...] + p_self)
        w = (a_fin * acc_ref[...] + p_self * vn) * inv_l
        a_x = pltpu.roll(a_fin, HEAD_QK, 1)
        p_x = pltpu.roll(p_self, HEAD_QK, 1)
        wx = (a_x * accx_ref[...] + p_x * vn) * pltpu.roll(inv_l, HEAD_QK, 1)
        low = lax.broadcasted_iota(jnp.int32, w.shape, 1) < HEAD_QK
        lam = _lambda_full(lq1_ref, lk1_ref, lq2_ref, lk2_ref)
        d = jnp.where(low, w, wx) - lam * jnp.where(low, wx, w)
        on = _rms_rows(d, sub_ref[...]) * (1.0 - LAM_INIT)
        g_ref[0] = on * zs_ref[0]


def _attn_decode(page_table, q_s, cache_k, cache_v, k_new, v_new, zs, subln, hs, lams, *, n_pg=4):
    dec_batch, n_pages = page_table.shape
    page = cache_k.shape[2]
    assert n_pages % n_pg == 0
    per_b = pl.BlockSpec((1, N_HEADS, HEAD_V), lambda b, p, pt: (b, 0, 0))
    page_specs = [
        pl.BlockSpec((1, 1, page, N_HEADS, HEAD_V),
                     lambda b, p, pt, i=i: (0, pt[b, p * n_pg + i], 0, 0, 0))
        for i in range(n_pg)]
    vec = pl.BlockSpec((1, HEAD_QK), lambda b, p, pt: (0, 0))
    acc = pltpu.VMEM((N_HEADS, HEAD_V), F32)
    page_bytes = page * N_HEADS * HEAD_V * 4
    vmem = n_pg * (4 * page_bytes + 5 * page_bytes) + (8 << 20)
    return pl.pallas_call(
        functools.partial(_attn_decode_kernel, page=page, n_pg=n_pg),
        grid_spec=pltpu.PrefetchScalarGridSpec(
            num_scalar_prefetch=1,
            grid=(dec_batch, n_pages // n_pg),
            in_specs=[per_b] + page_specs + page_specs + [
                per_b, per_b, per_b,
                pl.BlockSpec((1, HEAD_V), lambda b, p, pt: (0, 0)),
                pl.BlockSpec((HEAD_V, HEAD_V), lambda b, p, pt: (0, 0)),
                vec, vec, vec, vec,
            ],
            out_specs=per_b,
            scratch_shapes=[acc, acc, acc, acc],
        ),
        out_shape=jax.ShapeDtypeStruct((dec_batch, N_HEADS, HEAD_V), F32),
        compiler_params=_params(("arbitrary", "arbitrary"), vmem),
        name="attn_decode",
    )(page_table, q_s, *([cache_k] * n_pg), *([cache_v] * n_pg), k_new, v_new, zs, subln, hs, *lams)


def _out_b_kernel(g_ref, h_ref, w_ref, y_ref):
    y_ref[...] = h_ref[...] + _dot(g_ref[...].astype(BF16), w_ref[...])


def _out_b(g, h, w_out, *, tm):
    rows = g.shape[0]
    vmem = 2 * tm * (V_COLS * 4 + 2 * D_MODEL * 4) + w_out.size * 2 + 2 * tm * D_MODEL * 4 + (8 << 20)
    return pl.pallas_call(
        _out_b_kernel,
        grid=(rows // tm,),
        in_specs=[
            pl.BlockSpec((tm, V_COLS), lambda t: (t, 0)),
            pl.BlockSpec((tm, D_MODEL), lambda t: (t, 0)),
            _resident(w_out.shape),
        ],
        out_specs=pl.BlockSpec((tm, D_MODEL), lambda t: (t, 0)),
        out_shape=jax.ShapeDtypeStruct((rows, D_MODEL), F32),
        compiler_params=_params(("arbitrary",), vmem),
        name="out_b",
    )(g, h, w_out)


def kernel(x_prompt, x_sample, state_conv, cache_k, cache_v, page_table, meta_tokens, norm_a, w_in_a, conv_w_a, w_out_a, norm_b, w_in_b, q_norm_b, k_norm_b, lambda_q1_b, lambda_k1_b, lambda_q2_b, lambda_k2_b, subln_b, w_out_b):
    batch, seq, _ = x_prompt.shape
    dec_batch, dec_seq, _ = x_sample.shape
    n_meta = meta_tokens.shape[0]
    assert norm_a.shape[0] == 1 and norm_b.shape[0] == 1 and dec_seq == 1 and n_meta == N_META
    n_pages = page_table.shape[1]
    page = cache_k.shape[2]
    past_len = n_pages * page

    w_in_a_bf = w_in_a[0].astype(BF16)
    w_out_a_bf = w_out_a[0].astype(BF16)
    w_in_b_bf = w_in_b[0].astype(BF16)
    w_out_b_bf = w_out_b[0].astype(BF16)
    gain_a = norm_a[0][None, :]
    gain_b = norm_b[0][None, :]

    x_small = jnp.concatenate([meta_tokens, x_sample[:, 0, :]], axis=0)
    zpad = jnp.zeros((n_meta, D_CONV), F32)
    p1 = jnp.concatenate([zpad, state_conv[0, :, 1, :]], axis=0)
    p2 = jnp.concatenate([zpad, state_conv[0, :, 0, :]], axis=0)
    h_small, u_small = _mixer_a_small(x_small, gain_a, w_in_a_bf, conv_w_a[0], w_out_a_bf,
                                      p1, p2, n_meta=n_meta)
    init8 = u_small[n_meta - V7X_SUBLANES:n_meta]
    h_body, st_body = _mixer_a_body(x_prompt.reshape(batch * seq, D_MODEL), gain_a, w_in_a_bf,
                                    conv_w_a[0], w_out_a_bf, init8, batch=batch, seq=seq)
    conv_prompt = st_body[:, V7X_SUBLANES - (CONV_W - 1):, :][None]
    conv_sample = jnp.stack([state_conv[0, :, 1, :], u_small[n_meta:]], axis=1)[None]

    qn_t = jnp.tile(q_norm_b[0], N_MAPS)[None, :]
    kn_t = jnp.tile(k_norm_b[0], N_MAPS)[None, :]
    gi = np.arange(V7X_MXU_DIM) // HEAD_QK
    bd = jnp.asarray(gi[:, None] == gi[None, :], dtype=BF16)
    lanes1 = jnp.ones((1, V7X_LANES), F32)
    pos_body = (n_meta + jnp.arange(seq, dtype=F32))[:, None] * lanes1
    pos_small = jnp.concatenate([jnp.arange(n_meta, dtype=F32),
                                 jnp.full((dec_batch,), float(past_len), F32)])[:, None] * lanes1
    q_s, k_small, v_small, zs_small = _proj_b(
        h_small, pos_small, gain_b, w_in_b_bf, qn_t, kn_t, bd,
        batch=1, seq=n_meta + dec_batch, tm=n_meta + dec_batch, head_major=False)
    qt_hm, k_body, v_body, zs_body, k_hm, vt_hm = _proj_b(
        h_body, pos_body, gain_b, w_in_b_bf, qn_t, kn_t, bd,
        batch=batch, seq=seq, tm=ATTN_TILE, head_major=True)

    lams = tuple(a[0][None, :] for a in (lambda_q1_b, lambda_k1_b, lambda_q2_b, lambda_k2_b))

    meta_heads = lambda a: a[:n_meta].reshape(n_meta, N_HEADS, HEAD_V).astype(BF16)
    g_body = _attn_body(qt_hm, k_hm, vt_hm, meta_heads(k_small).transpose(1, 0, 2),
                        meta_heads(v_small).transpose(1, 2, 0), zs_body,
                        subln_b[0][None, :], lams, batch=batch, seq=seq, tq=ATTN_TILE)
    y_body = _out_b(g_body, h_body, w_out_b_bf, tm=512)

    per_head = lambda a: a[n_meta:].reshape(dec_batch, N_HEADS, HEAD_V)
    li = np.arange(HEAD_V) // HEAD_QK
    half_sum = jnp.asarray(li[:, None] == li[None, :], dtype=BF16)
    g_dec = _attn_decode(page_table, per_head(q_s), cache_k, cache_v, per_head(k_small),
                         per_head(v_small), per_head(zs_small), subln_b[0][None, :], half_sum, lams)
    y_dec = _out_b(g_dec.reshape(dec_batch, V_COLS), h_small[n_meta:], w_out_b_bf, tm=dec_batch)

    def with_meta(small, body):
        meta = jnp.broadcast_to(small[:n_meta].reshape(1, n_meta, N_HEADS, HEAD_V),
                                (batch, n_meta, N_HEADS, HEAD_V))
        return jnp.concatenate([meta, body.reshape(batch, seq, N_HEADS, HEAD_V)], axis=1)[None]

    return (
        y_body.reshape(batch, seq, D_MODEL),
        y_dec.reshape(dec_batch, 1, D_MODEL),
        conv_prompt,
        conv_sample,
        with_meta(k_small, k_body),
        with_meta(v_small, v_body),
        k_small[n_meta:].reshape(1, dec_batch, 1, N_HEADS, HEAD_V),
        v_small[n_meta:].reshape(1, dec_batch, 1, N_HEADS, HEAD_V),
    )
```

```python
import functools
import math

import numpy as np
import jax
import jax.numpy as jnp
from jax import lax
from jax.experimental import pallas as pl
from jax.experimental.pallas import tpu as pltpu

D_MODEL = 1024
D_CONV = 2048
CONV_W = 3
N_HEADS = 16
HEAD_QK = 64
HEAD_V = 128
ROT_DIM = 16
ROPE_THETA = 500000.0
EPS = 1e-6
N_META = 16
QK_COLS = N_HEADS * 2 * HEAD_QK
V_COLS = N_HEADS * HEAD_V
N_MAPS = 2 * N_HEADS
LAM_INIT = 0.8 - 0.6 * math.exp(-0.3 * 1)
SCORE_SCALE = HEAD_QK ** -0.5
Q_SCALE = SCORE_SCALE * math.log2(math.e)
LOG2_HEAD_QK = HEAD_QK.bit_length() - 1
LOG2_HEAD_V = HEAD_V.bit_length() - 1
LOG2_N_HEADS = N_HEADS.bit_length() - 1

V7X_LANES = 128
V7X_SUBLANES = 8
V7X_MXU_DIM = 256
V7X_VMEM_BYTES = 64 * 1024 * 1024

ATTN_TILE = 512
DECODE_CHUNK = 16

F32 = jnp.float32
BF16 = jnp.bfloat16


def _dot(a, b):
    return jnp.dot(a, b, preferred_element_type=F32)


def _rms_rows(x, gain):
    ms = jnp.mean(x * x, axis=-1, keepdims=True)
    return x * lax.rsqrt(ms + EPS) * gain


def _silu(z):
    return z * jax.nn.sigmoid(z)


def _lambda_full(lq1_ref, lk1_ref, lq2_ref, lk2_ref):
    a = jnp.sum(lq1_ref[...] * lk1_ref[...], axis=-1, keepdims=True)
    b = jnp.sum(lq2_ref[...] * lk2_ref[...], axis=-1, keepdims=True)
    return jnp.exp(a) - jnp.exp(b) + LAM_INIT


def _params(semantics, vmem_bytes):
    return pltpu.CompilerParams(dimension_semantics=semantics,
                                vmem_limit_bytes=min(int(vmem_bytes), V7X_VMEM_BYTES - (4 << 20)))


def _resident(shape):
    return pl.BlockSpec(shape, lambda *_: (0,) * len(shape), pipeline_mode=pl.Buffered(1))


def _conv_gate_out(xn, u, um1, um2, wb, wz, cw, wout):
    conv = cw[0:1, :] * um2 + cw[1:2, :] * um1 + cw[2:3, :] * u
    y = _dot(xn, wb) * conv * _silu(_dot(xn, wz))
    return _dot(y.astype(BF16), wout)


def _mixer_a_body_kernel(x_ref, g_ref, win_ref, cw_ref, wout_ref, init_ref,
                         hp_ref, st_ref, carry_ref, *, tm, tc):
    @pl.when(pl.program_id(1) == 0)
    def _():
        carry_ref[...] = init_ref[...]

    x = x_ref[...]
    xn = _rms_rows(x, g_ref[...]).astype(BF16)
    row = lax.broadcasted_iota(jnp.int32, (tm, tc), 0)
    acc = jnp.zeros((tm, D_MODEL), F32)
    for j in range(D_CONV // tc):
        lo = j * tc
        u = _dot(xn, win_ref[:, lo:lo + tc]) * _dot(xn, win_ref[:, 2 * D_CONV + lo:2 * D_CONV + lo + tc])
        prev = carry_ref[:, lo:lo + tc]
        p1 = prev[7:8, :]
        p2 = prev[6:7, :]
        um1 = jnp.where(row == 0, p1, pltpu.roll(u, 1, 0))
        um2 = jnp.where(row == 0, p2, jnp.where(row == 1, p1, pltpu.roll(u, 2, 0)))
        carry_ref[:, lo:lo + tc] = u[tm - V7X_SUBLANES:tm, :]
        acc = acc + _conv_gate_out(
            xn, u, um1, um2,
            win_ref[:, D_CONV + lo:D_CONV + lo + tc],
            win_ref[:, 3 * D_CONV + lo:3 * D_CONV + lo + tc],
            cw_ref[:, lo:lo + tc], wout_ref[lo:lo + tc, :])
    hp_ref[...] = x + acc
    st_ref[0] = carry_ref[...]


def _mixer_a_small_kernel(x_ref, g_ref, wc_ref, wb_ref, wv_ref, wz_ref, cw_ref, wout_ref,
                          p1_ref, p2_ref, hp_ref, u_ref, *, n_meta):
    x = x_ref[...]
    xn = _rms_rows(x, g_ref[...]).astype(BF16)
    u = _dot(xn, wc_ref[...]) * _dot(xn, wv_ref[...])
    row = lax.broadcasted_iota(jnp.int32, u.shape, 0)
    um1 = jnp.where(row == 0, 0.0, jnp.where(row < n_meta, pltpu.roll(u, 1, 0), p1_ref[...]))
    um2 = jnp.where(row < 2, 0.0, jnp.where(row < n_meta, pltpu.roll(u, 2, 0), p2_ref[...]))
    u_ref[...] = u
    contrib = _conv_gate_out(xn, u, um1, um2, wb_ref[...], wz_ref[...], cw_ref[...], wout_ref[...])

    @pl.when(pl.program_id(0) == 0)
    def _():
        hp_ref[...] = x + contrib

    @pl.when(pl.program_id(0) != 0)
    def _():
        hp_ref[...] += contrib


def _mixer_a_body(x2d, gain, w_in, conv_w, w_out, init8, *, batch, seq, tm=512, tc=512):
    nt = seq // tm
    vmem = (4 * tm * D_MODEL * 4
            + w_in.size * 2 + w_out.size * 2
            + 10 * tm * tc * 4 + 2 * tm * D_MODEL * 4
            + (8 << 20))
    return pl.pallas_call(
        functools.partial(_mixer_a_body_kernel, tm=tm, tc=tc),
        grid=(batch, nt),
        in_specs=[
            pl.BlockSpec((tm, D_MODEL), lambda b, t: (b * nt + t, 0)),
            _resident((1, D_MODEL)),
            _resident(w_in.shape),
            _resident(conv_w.shape),
            _resident(w_out.shape),
            _resident(init8.shape),
        ],
        out_specs=[
            pl.BlockSpec((tm, D_MODEL), lambda b, t: (b * nt + t, 0)),
            pl.BlockSpec((1, V7X_SUBLANES, D_CONV), lambda b, t: (b, 0, 0)),
        ],
        out_shape=[
            jax.ShapeDtypeStruct((batch * seq, D_MODEL), F32),
            jax.ShapeDtypeStruct((batch, V7X_SUBLANES, D_CONV), F32),
        ],
        scratch_shapes=[pltpu.VMEM((V7X_SUBLANES, D_CONV), F32)],
        compiler_params=_params(("arbitrary", "arbitrary"), vmem),
        name="mixer_a_body",
    )(x2d, gain, w_in, conv_w, w_out, init8)


def _mixer_a_small(x, gain, w_in, conv_w, w_out, p1, p2, *, n_meta, tc=512):
    rows = x.shape[0]
    nj = D_CONV // tc
    w_spec = lambda k: pl.BlockSpec((D_MODEL, tc), lambda j, k=k: (0, k * nj + j))
    vmem = 2 * (4 * D_MODEL * tc * 2 + tc * D_MODEL * 2) + (8 << 20)
    return pl.pallas_call(
        functools.partial(_mixer_a_small_kernel, n_meta=n_meta),
        grid=(nj,),
        in_specs=[
            pl.BlockSpec((rows, D_MODEL), lambda j: (0, 0)),
            pl.BlockSpec((1, D_MODEL), lambda j: (0, 0)),
            w_spec(0), w_spec(1), w_spec(2), w_spec(3),
            pl.BlockSpec((CONV_W, tc), lambda j: (0, j)),
            pl.BlockSpec((tc, D_MODEL), lambda j: (j, 0)),
            pl.BlockSpec((rows, tc), lambda j: (0, j)),
            pl.BlockSpec((rows, tc), lambda j: (0, j)),
        ],
        out_specs=[
            pl.BlockSpec((rows, D_MODEL), lambda j: (0, 0)),
            pl.BlockSpec((rows, tc), lambda j: (0, j)),
        ],
        out_shape=[
            jax.ShapeDtypeStruct((rows, D_MODEL), F32),
            jax.ShapeDtypeStruct((rows, D_CONV), F32),
        ],
        compiler_params=_params(("arbitrary",), vmem),
        name="mixer_a_small",
    )(x, gain, w_in, w_in, w_in, w_in, conv_w, w_out, p1, p2)


def _rope_tables(pos, tc):
    lane = lax.broadcasted_iota(jnp.int32, (1, V7X_LANES), 1)
    l64 = lane & (HEAD_QK - 1)
    half = ROT_DIM // 2
    idx = jnp.where(l64 < half, l64, l64 - half).astype(F32)
    inv = jnp.where(l64 < ROT_DIM, jnp.exp(idx * (-2.0 / ROT_DIM * math.log(ROPE_THETA))), 0.0)
    ang = pos * inv
    cos = jnp.cos(ang)
    sin = jnp.sin(ang)
    lo = jnp.where(l64 < half, -sin, 0.0)
    hi = jnp.where((l64 >= half) & (l64 < ROT_DIM), sin, 0.0)
    reps = tc // V7X_LANES
    return tuple(jnp.concatenate([t] * reps, axis=1) for t in (cos, lo, hi))


def _norm_rope(p, gain, bd, tables):
    tc = p.shape[1]
    sq = (p * p).astype(BF16)
    ss = jnp.concatenate(
        [_dot(sq[:, i * V7X_MXU_DIM:(i + 1) * V7X_MXU_DIM], bd) for i in range(tc // V7X_MXU_DIM)],
        axis=1)
    pn = p * lax.rsqrt(ss * (1.0 / HEAD_QK) + EPS) * gain
    cos, lo, hi = tables
    half = ROT_DIM // 2
    return pn * cos + pltpu.roll(pn, tc - half, 1) * lo + pltpu.roll(pn, half, 1) * hi


def _proj_b_kernel(h_ref, pos_ref, g_ref, win_ref, qn_ref, kn_ref, bd_ref, *out_refs,
                   tc, head_major):
    xn = _rms_rows(h_ref[...], g_ref[...]).astype(BF16)
    tables = _rope_tables(pos_ref[...], tc)
    bd = bd_ref[...]
    heads_per_chunk = tc // HEAD_V
    if head_major:
        q_ref, kf_ref, vf_ref, zs_ref, kb_ref, vt_ref = out_refs
    else:
        q_ref, kf_ref, vf_ref, zs_ref = out_refs
    for j in range(QK_COLS // tc):
        lo = j * tc
        q = _norm_rope(_dot(xn, win_ref[:, lo:lo + tc]), qn_ref[:, lo:lo + tc], bd, tables)
        q = q * Q_SCALE
        k = _norm_rope(_dot(xn, win_ref[:, QK_COLS + lo:QK_COLS + lo + tc]),
                       kn_ref[:, lo:lo + tc], bd, tables)
        v = _dot(xn, win_ref[:, 2 * QK_COLS + lo:2 * QK_COLS + lo + tc])
        z = _dot(xn, win_ref[:, 2 * QK_COLS + V_COLS + lo:2 * QK_COLS + V_COLS + lo + tc])
        kf_ref[:, lo:lo + tc] = k
        vf_ref[:, lo:lo + tc] = v
        zs_ref[:, lo:lo + tc] = _silu(z).astype(zs_ref.dtype)
        if head_major:
            for hh in range(heads_per_chunk):
                h = j * heads_per_chunk + hh
                sl = slice(hh * HEAD_V, (hh + 1) * HEAD_V)
                q_ref[0, h, 0] = q[:, sl].T.astype(BF16)
                kb_ref[0, h] = k[:, sl].astype(BF16)
                vt_ref[0, h, 0] = v[:, sl].T.astype(BF16)
        else:
            q_ref[:, lo:lo + tc] = q


def _proj_b(h2d, pos, gain, w_in, qn, kn, bd, *, batch, seq, tm, head_major, tc=512):
    nt = seq // tm
    rows = batch * seq
    row_spec = lambda cols: pl.BlockSpec((tm, cols), lambda b, t: (b * nt + t, 0))
    hm_spec = pl.BlockSpec((1, N_HEADS, tm, HEAD_V), lambda b, t: (b, 0, t, 0))
    hm_shape = jax.ShapeDtypeStruct((batch, N_HEADS, seq, HEAD_V), BF16)
    tr_spec = pl.BlockSpec((1, N_HEADS, 1, HEAD_V, tm), lambda b, t: (b, 0, t, 0, 0))
    tr_shape = jax.ShapeDtypeStruct((batch, N_HEADS, nt, HEAD_V, tm), BF16)
    row_shape = lambda dt: jax.ShapeDtypeStruct((rows, QK_COLS), dt)
    if head_major:
        out_specs = [tr_spec, row_spec(QK_COLS), row_spec(V_COLS), row_spec(V_COLS), hm_spec, tr_spec]
        out_shape = [tr_shape, row_shape(F32), row_shape(F32), row_shape(BF16), hm_shape, tr_shape]
    else:
        out_specs = [row_spec(QK_COLS), row_spec(QK_COLS), row_spec(V_COLS), row_spec(V_COLS)]
        out_shape = [row_shape(F32), row_shape(F32), row_shape(F32), row_shape(F32)]
    vmem = (2 * tm * D_MODEL * 4 + w_in.size * 2
            + 2 * tm * QK_COLS * (4 + 4 + 4 + 4 + 4)
            + 12 * tm * tc * 4 + (8 << 20))
    return pl.pallas_call(
        functools.partial(_proj_b_kernel, tc=tc, head_major=head_major),
        grid=(batch, nt),
        in_specs=[
            row_spec(D_MODEL),
            pl.BlockSpec((tm, V7X_LANES), lambda b, t: (t, 0)),
            _resident((1, D_MODEL)),
            _resident(w_in.shape),
            _resident((1, QK_COLS)),
            _resident((1, QK_COLS)),
            _resident(bd.shape),
        ],
        out_specs=out_specs,
        out_shape=out_shape,
        compiler_params=_params(("arbitrary", "arbitrary"), vmem),
        name="proj_b_body" if head_major else "proj_b_small",
    )(h2d, pos, gain, w_in, qn, kn, bd)


def _online_update(state, st, vt):
    m, l, acc = state
    m_new = jnp.maximum(m, jnp.max(st, axis=0, keepdims=True))
    alpha = jnp.exp2(m - m_new)
    pt = jnp.exp2(st - m_new)
    return (m_new, alpha * l + jnp.sum(pt, axis=0, keepdims=True),
            alpha * acc + _dot(vt, pt.astype(BF16)))


def _attn_full_blocks(i, qt, k_ref, vt_ref, *, tq):
    drow = lax.broadcasted_iota(jnp.int32, qt.shape, 0)
    zero = jnp.zeros_like(qt)
    qtm = (jnp.where(drow < HEAD_QK, qt, zero), jnp.where(drow >= HEAD_QK, qt, zero))

    def scores(j, rows=None):
        kb = k_ref[0, 0, pl.ds(pl.multiple_of(j * tq, tq), tq), :] if rows is None else rows
        return tuple(_dot(kb, qtm[c]) for c in range(2))

    empty = (jnp.full((1, tq), -jnp.inf, F32), jnp.zeros((1, tq), F32), jnp.zeros((HEAD_V, tq), F32))
    n_odd = i & 1

    def one_block(j, states):
        sts = scores(j)
        vt = vt_ref[0, 0, j]
        return tuple(_online_update(states[c], sts[c], vt) for c in range(2))

    def two_blocks(p, states):
        ja = n_odd + 2 * p
        sa = scores(ja)
        sb = scores(ja + 1)
        va = vt_ref[0, 0, ja]
        vb = vt_ref[0, 0, ja + 1]
        states = tuple(_online_update(states[c], sa[c], va) for c in range(2))
        return tuple(_online_update(states[c], sb[c], vb) for c in range(2))

    states = lax.fori_loop(0, n_odd, one_block, (empty, empty))
    states = lax.fori_loop(0, lax.shift_right_logical(i, 1), two_blocks, states)
    return states, scores


def _attn_last_blocks(i, states, scores, vt_ref, kp, vpt, lam, *, tq):
    causal = (lax.broadcasted_iota(jnp.int32, (tq, tq), 0)
              <= lax.broadcasted_iota(jnp.int32, (tq, tq), 1))
    sts = scores(i)
    sps = scores(None, rows=kp)
    vt = vt_ref[0, 0, i]
    states = tuple(
        _online_update(states[c], jnp.where(causal, sts[c], -jnp.inf), vt) for c in range(2))
    (_, l0, a0), (_, l1, a1) = tuple(_online_update(states[c], sps[c], vpt) for c in range(2))
    return (a0 * (1.0 / l0) - lam * (a1 * (1.0 / l1))).T


def _half_sums(x):
    lane = lax.broadcasted_iota(jnp.int32, x.shape, 1)
    low = lane < HEAD_QK
    s_lo = jnp.sum(jnp.where(low, x, 0.0), axis=1, keepdims=True)
    s_hi = jnp.sum(jnp.where(low, 0.0, x), axis=1, keepdims=True)
    return jnp.where(low, s_lo, s_hi)


def _decode_reset(m_ref, l_ref, acc_ref, accx_ref):
    m_ref[...] = jnp.full(m_ref.shape, -jnp.inf, F32)
    l_ref[...] = jnp.zeros(l_ref.shape, F32)
    acc_ref[...] = jnp.zeros(acc_ref.shape, F32)
    accx_ref[...] = jnp.zeros(accx_ref.shape, F32)


def _decode_pages(q, hs, k_refs, v_refs, m_ref, l_ref, acc_ref, accx_ref, *, page):
    n_pg = len(k_refs)

    def page_scores(k_ref):
        prod = (k_ref[0, 0] * q[None]).astype(BF16).reshape(page * N_HEADS, HEAD_V)
        return _dot(prod, hs).reshape(page, N_HEADS, HEAD_V)

    m = m_ref[...]
    l = l_ref[...]
    acc = acc_ref[...]
    accx = accx_ref[...]
    s_cur = page_scores(k_refs[0])
    for i in range(n_pg):
        s_next = page_scores(k_refs[i + 1]) if i + 1 < n_pg else None
        m_new = jnp.maximum(m, jnp.max(s_cur, axis=0))
        alpha = jnp.exp2(m - m_new)
        l = alpha * l
        acc = alpha * acc
        accx = pltpu.roll(alpha, HEAD_QK, 1) * accx
        for t0 in range(0, page, DECODE_CHUNK):
            p3 = jnp.exp2(s_cur[t0:t0 + DECODE_CHUNK] - m_new[None])
            px3 = pltpu.roll(p3.reshape(DECODE_CHUNK * N_HEADS, HEAD_V), HEAD_QK, 1).reshape(p3.shape)
            v3 = v_refs[i][0, 0, t0:t0 + DECODE_CHUNK]
            l = l + jnp.sum(p3, axis=0)
            acc = acc + jnp.sum(v3 * p3, axis=0)
            accx = accx + jnp.sum(v3 * px3, axis=0)
        m = m_new
        s_cur = s_next
    m_ref[...] = m
    l_ref[...] = l
    acc_ref[...] = acc
    accx_ref[...] = accx


def _decode_finish(q, kn, vn, zs, sub, lam, m_ref, l_ref, acc_ref, accx_ref):
    s_self = _half_sums(q * kn)
    m_old = m_ref[...]
    m_fin = jnp.maximum(m_old, s_self)
    a_fin = jnp.exp2(m_old - m_fin)
    p_self = jnp.exp2(s_self - m_fin)
    inv_l = 1.0 / (a_fin * l_ref[...] + p_self)
    w = (a_fin * acc_ref[...] + p_self * vn) * inv_l
    a_x = pltpu.roll(a_fin, HEAD_QK, 1)
    p_x = pltpu.roll(p_self, HEAD_QK, 1)
    wx = (a_x * accx_ref[...] + p_x * vn) * pltpu.roll(inv_l, HEAD_QK, 1)
    low = lax.broadcasted_iota(jnp.int32, w.shape, 1) < HEAD_QK
    d = jnp.where(low, w, wx) - lam * jnp.where(low, wx, w)
    return _rms_rows(d, sub) * (1.0 - LAM_INIT) * zs


def _attn_fused_kernel(pt_ref, qt_ref, k_ref, vt_ref, kp_ref, vpt_ref, zs_ref, sub_ref,
                       lq1_ref, lk1_ref, lq2_ref, lk2_ref, qd_ref, *refs, tq, page, n_pg, groups):
    del pt_ref
    k_pages = refs[:n_pg]
    v_pages = refs[n_pg:2 * n_pg]
    (kn_ref, vn_ref, zsd_ref, hs_ref, g_ref, gd_ref,
     m_ref, l_ref, acc_ref, accx_ref) = refs[2 * n_pg:]
    i = pl.program_id(2)
    step = (pl.program_id(0) * pl.num_programs(1) + pl.program_id(1)) * pl.num_programs(2) + i
    grp = lax.rem(step, groups)
    dec_state = (m_ref, l_ref, acc_ref, accx_ref)

    @pl.when(grp == 0)
    def _():
        _decode_reset(*dec_state)

    qt = qt_ref[0, 0, 0]
    states, scores = _attn_full_blocks(i, qt, k_ref, vt_ref, tq=tq)

    qd = qd_ref[0]
    _decode_pages(qd, hs_ref[...], k_pages, v_pages, *dec_state, page=page)
    lam = _lambda_full(lq1_ref, lk1_ref, lq2_ref, lk2_ref)
    o = _attn_last_blocks(i, states, scores, vt_ref, kp_ref[0], vpt_ref[0], lam, tq=tq)
    on = _rms_rows(o, sub_ref[...]) * (1.0 - LAM_INIT)
    g_ref[...] = (on * zs_ref[...].astype(F32)).astype(g_ref.dtype)

    @pl.when(grp == groups - 1)
    def _():
        gd_ref[0] = _decode_finish(qd, kn_ref[0], vn_ref[0], zsd_ref[0], sub_ref[...], lam, *dec_state)


def _attn_fused(page_table, qt, kb, vt, kp, vpt, zs, subln, lams,
                q_s, cache_k, cache_v, k_new, v_new, zs_s, hs, *, batch, seq, tq, n_pg=4):
    nq = seq // tq
    dec_batch, n_pages = page_table.shape
    page = cache_k.shape[2]
    groups = n_pages // n_pg
    assert n_pages % n_pg == 0 and batch * N_HEADS * nq == dec_batch * groups

    def dec_pos(b, h, i):
        step = (b * N_HEADS + h) * nq + i
        return step // groups, step % groups

    per_row = pl.BlockSpec((1, N_HEADS, HEAD_V), lambda b, h, i, pt: (dec_pos(b, h, i)[0], 0, 0))

    def page_spec(j):
        def index(b, h, i, pt):
            row, grp = dec_pos(b, h, i)
            return (0, pt[row, grp * n_pg + j], 0, 0, 0)
        return pl.BlockSpec((1, 1, page, N_HEADS, HEAD_V), index)

    page_specs = [page_spec(j) for j in range(n_pg)]
    tr_head = pl.BlockSpec((1, 1, nq, HEAD_V, tq), lambda b, h, i, pt: (b, h, 0, 0, 0))
    vec = pl.BlockSpec((1, HEAD_QK), lambda b, h, i, pt: (0, 0))
    tile = pl.BlockSpec((tq, HEAD_V), lambda b, h, i, pt: (b * nq + i, h))
    acc = pltpu.VMEM((N_HEADS, HEAD_V), F32)
    page_bytes = page * N_HEADS * HEAD_V * 4
    vmem = (n_pg * (4 * page_bytes + 5 * page_bytes)
            + 4 * seq * HEAD_V * 2 + 10 * tq * tq * 4 + (8 << 20))
    return pl.pallas_call(
        functools.partial(_attn_fused_kernel, tq=tq, page=page, n_pg=n_pg, groups=groups),
        grid_spec=pltpu.PrefetchScalarGridSpec(
            num_scalar_prefetch=1,
            grid=(batch, N_HEADS, nq),
            in_specs=[
                pl.BlockSpec((1, 1, 1, HEAD_V, tq), lambda b, h, i, pt: (b, h, i, 0, 0)),
                pl.BlockSpec((1, 1, seq, HEAD_V), lambda b, h, i, pt: (b, h, 0, 0)),
                tr_head,
                pl.BlockSpec((1, N_META, HEAD_V), lambda b, h, i, pt: (h, 0, 0)),
                pl.BlockSpec((1, HEAD_V, N_META), lambda b, h, i, pt: (h, 0, 0)),
                tile,
                pl.BlockSpec((1, HEAD_V), lambda b, h, i, pt: (0, 0)),
                vec, vec, vec, vec,
                per_row,
            ] + page_specs + page_specs + [
                per_row, per_row, per_row,
                pl.BlockSpec((HEAD_V, HEAD_V), lambda b, h, i, pt: (0, 0)),
            ],
            out_specs=[tile, per_row],
            scratch_shapes=[acc, acc, acc, acc],
        ),
        out_shape=[jax.ShapeDtypeStruct((batch * seq, V_COLS), BF16),
                   jax.ShapeDtypeStruct((dec_batch, N_HEADS, HEAD_V), F32)],
        compiler_params=_params(("arbitrary", "arbitrary", "arbitrary"), vmem),
        name="attn_fused",
    )(page_table, qt, kb, vt, kp, vpt, zs, subln, *lams,
      q_s, *([cache_k] * n_pg), *([cache_v] * n_pg), k_new, v_new, zs_s, hs)


def _out_b_kernel(g_ref, h_ref, w_ref, y_ref):
    y_ref[...] = h_ref[...] + _dot(g_ref[...].astype(BF16), w_ref[...])


def _out_b(g, h, w_out, *, tm):
    rows = g.shape[0]
    vmem = 2 * tm * (V_COLS * 4 + 2 * D_MODEL * 4) + w_out.size * 2 + 2 * tm * D_MODEL * 4 + (8 << 20)
    return pl.pallas_call(
        _out_b_kernel,
        grid=(rows // tm,),
        in_specs=[
            pl.BlockSpec((tm, V_COLS), lambda t: (t, 0)),
            pl.BlockSpec((tm, D_MODEL), lambda t: (t, 0)),
            _resident(w_out.shape),
        ],
        out_specs=pl.BlockSpec((tm, D_MODEL), lambda t: (t, 0)),
        out_shape=jax.ShapeDtypeStruct((rows, D_MODEL), F32),
        compiler_params=_params(("arbitrary",), vmem),
        name="out_b",
    )(g, h, w_out)


def kernel(x_prompt, x_sample, state_conv, cache_k, cache_v, page_table, meta_tokens, norm_a, w_in_a, conv_w_a, w_out_a, norm_b, w_in_b, q_norm_b, k_norm_b, lambda_q1_b, lambda_k1_b, lambda_q2_b, lambda_k2_b, subln_b, w_out_b):
    batch, seq, _ = x_prompt.shape
    dec_batch, dec_seq, _ = x_sample.shape
    n_meta = meta_tokens.shape[0]
    assert norm_a.shape[0] == 1 and norm_b.shape[0] == 1 and dec_seq == 1 and n_meta == N_META
    n_pages = page_table.shape[1]
    page = cache_k.shape[2]
    past_len = n_pages * page

    w_in_a_bf = w_in_a[0].astype(BF16)
    w_out_a_bf = w_out_a[0].astype(BF16)
    w_in_b_bf = w_in_b[0].astype(BF16)
    w_out_b_bf = w_out_b[0].astype(BF16)
    gain_a = norm_a[0][None, :]
    gain_b = norm_b[0][None, :]

    x_small = jnp.concatenate([meta_tokens, x_sample[:, 0, :]], axis=0)
    zpad = jnp.zeros((n_meta, D_CONV), F32)
    p1 = jnp.concatenate([zpad, state_conv[0, :, 1, :]], axis=0)
    p2 = jnp.concatenate([zpad, state_conv[0, :, 0, :]], axis=0)
    h_small, u_small = _mixer_a_small(x_small, gain_a, w_in_a_bf, conv_w_a[0], w_out_a_bf,
                                      p1, p2, n_meta=n_meta)
    init8 = u_small[n_meta - V7X_SUBLANES:n_meta]
    h_body, st_body = _mixer_a_body(x_prompt.reshape(batch * seq, D_MODEL), gain_a, w_in_a_bf,
                                    conv_w_a[0], w_out_a_bf, init8, batch=batch, seq=seq)
    conv_prompt = st_body[:, V7X_SUBLANES - (CONV_W - 1):, :][None]
    conv_sample = jnp.stack([state_conv[0, :, 1, :], u_small[n_meta:]], axis=1)[None]

    qn_t = jnp.tile(q_norm_b[0], N_MAPS)[None, :]
    kn_t = jnp.tile(k_norm_b[0], N_MAPS)[None, :]
    gi = np.arange(V7X_MXU_DIM) // HEAD_QK
    bd = jnp.asarray(gi[:, None] == gi[None, :], dtype=BF16)
    lanes1 = jnp.ones((1, V7X_LANES), F32)
    pos_body = (n_meta + jnp.arange(seq, dtype=F32))[:, None] * lanes1
    pos_small = jnp.concatenate([jnp.arange(n_meta, dtype=F32),
                                 jnp.full((dec_batch,), float(past_len), F32)])[:, None] * lanes1
    q_s, k_small, v_small, zs_small = _proj_b(
        h_small, pos_small, gain_b, w_in_b_bf, qn_t, kn_t, bd,
        batch=1, seq=n_meta + dec_batch, tm=n_meta + dec_batch, head_major=False)
    qt_hm, k_body, v_body, zs_body, k_hm, vt_hm = _proj_b(
        h_body, pos_body, gain_b, w_in_b_bf, qn_t, kn_t, bd,
        batch=batch, seq=seq, tm=ATTN_TILE, head_major=True)

    lams = tuple(a[0][None, :] for a in (lambda_q1_b, lambda_k1_b, lambda_q2_b, lambda_k2_b))

    meta_heads = lambda a: a[:n_meta].reshape(n_meta, N_HEADS, HEAD_V).astype(BF16)
    per_head = lambda a: a[n_meta:].reshape(dec_batch, N_HEADS, HEAD_V)
    li = np.arange(HEAD_V) // HEAD_QK
    half_sum = jnp.asarray(li[:, None] == li[None, :], dtype=BF16)
    g_body, g_dec = _attn_fused(
        page_table, qt_hm, k_hm, vt_hm, meta_heads(k_small).transpose(1, 0, 2),
        meta_heads(v_small).transpose(1, 2, 0), zs_body, subln_b[0][None, :], lams,
        per_head(q_s), cache_k, cache_v, per_head(k_small), per_head(v_small), per_head(zs_small),
        half_sum, batch=batch, seq=seq, tq=ATTN_TILE)
    y_body = _out_b(g_body, h_body, w_out_b_bf, tm=512)
    y_dec = _out_b(g_dec.reshape(dec_batch, V_COLS), h_small[n_meta:], w_out_b_bf, tm=dec_batch)

    def with_meta(small, body):
        meta = jnp.broadcast_to(small[:n_meta].reshape(1, n_meta, N_HEADS, HEAD_V),
                                (batch, n_meta, N_HEADS, HEAD_V))
        return jnp.concatenate([meta, body.reshape(batch, seq, N_HEADS, HEAD_V)], axis=1)[None]

    return (
        y_body.reshape(batch, seq, D_MODEL),
        y_dec.reshape(dec_batch, 1, D_MODEL),
        conv_prompt,
        conv_sample,
        with_meta(k_small, k_body),
        with_meta(v_small, v_body),
        k_small[n_meta:].reshape(1, dec_batch, 1, N_HEADS, HEAD_V),
        v_small[n_meta:].reshape(1, dec_batch, 1, N_HEADS, HEAD_V),
    )
```

```python
import functools
import math

import numpy as np
import jax
import jax.numpy as jnp
from jax import lax
from jax.experimental import pallas as pl
from jax.experimental.pallas import tpu as pltpu

D_MODEL = 1024
D_CONV = 2048
CONV_W = 3
N_HEADS = 16
HEAD_QK = 64
HEAD_V = 128
ROT_DIM = 16
ROPE_THETA = 500000.0
EPS = 1e-6
N_META = 16
QK_COLS = N_HEADS * 2 * HEAD_QK
V_COLS = N_HEADS * HEAD_V
N_MAPS = 2 * N_HEADS
LAM_INIT = 0.8 - 0.6 * math.exp(-0.3 * 1)
SCORE_SCALE = HEAD_QK ** -0.5
Q_SCALE = SCORE_SCALE * math.log2(math.e)
LOG2_HEAD_QK = HEAD_QK.bit_length() - 1
LOG2_HEAD_V = HEAD_V.bit_length() - 1
LOG2_N_HEADS = N_HEADS.bit_length() - 1

V7X_LANES = 128
V7X_SUBLANES = 8
V7X_MXU_DIM = 256
V7X_VMEM_BYTES = 64 * 1024 * 1024

ATTN_TILE = 512
DECODE_CHUNK = 16

F32 = jnp.float32
BF16 = jnp.bfloat16


def _dot(a, b):
    return jnp.dot(a, b, preferred_element_type=F32)


def _rms_rows(x, gain):
    ms = jnp.mean(x * x, axis=-1, keepdims=True)
    return x * lax.rsqrt(ms + EPS) * gain


def _silu(z):
    return z * jax.nn.sigmoid(z)


def _lambda_full(lq1_ref, lk1_ref, lq2_ref, lk2_ref):
    a = jnp.sum(lq1_ref[...] * lk1_ref[...], axis=-1, keepdims=True)
    b = jnp.sum(lq2_ref[...] * lk2_ref[...], axis=-1, keepdims=True)
    return jnp.exp(a) - jnp.exp(b) + LAM_INIT


def _params(semantics, vmem_bytes):
    return pltpu.CompilerParams(dimension_semantics=semantics,
                                vmem_limit_bytes=min(int(vmem_bytes), V7X_VMEM_BYTES - (4 << 20)))


def _resident(shape):
    return pl.BlockSpec(shape, lambda *_: (0,) * len(shape), pipeline_mode=pl.Buffered(1))


def _conv_gate_out(xn, u, um1, um2, wb, wz, cw, wout):
    conv = cw[0:1, :] * um2 + cw[1:2, :] * um1 + cw[2:3, :] * u
    y = _dot(xn, wb) * conv * _silu(_dot(xn, wz))
    return _dot(y.astype(BF16), wout)


def _mixer_a_body_kernel(x_ref, g_ref, win_ref, cw_ref, wout_ref, init_ref,
                         hp_ref, st_ref, carry_ref, *, tm, tc):
    @pl.when(pl.program_id(1) == 0)
    def _():
        carry_ref[...] = init_ref[...]

    x = x_ref[...]
    xn = _rms_rows(x, g_ref[...]).astype(BF16)
    row = lax.broadcasted_iota(jnp.int32, (tm, tc), 0)
    acc = jnp.zeros((tm, D_MODEL), F32)
    for j in range(D_CONV // tc):
        lo = j * tc
        u = _dot(xn, win_ref[:, lo:lo + tc]) * _dot(xn, win_ref[:, 2 * D_CONV + lo:2 * D_CONV + lo + tc])
        prev = carry_ref[:, lo:lo + tc]
        p1 = prev[7:8, :]
        p2 = prev[6:7, :]
        um1 = jnp.where(row == 0, p1, pltpu.roll(u, 1, 0))
        um2 = jnp.where(row == 0, p2, jnp.where(row == 1, p1, pltpu.roll(u, 2, 0)))
        carry_ref[:, lo:lo + tc] = u[tm - V7X_SUBLANES:tm, :]
        acc = acc + _conv_gate_out(
            xn, u, um1, um2,
            win_ref[:, D_CONV + lo:D_CONV + lo + tc],
            win_ref[:, 3 * D_CONV + lo:3 * D_CONV + lo + tc],
            cw_ref[:, lo:lo + tc], wout_ref[lo:lo + tc, :])
    hp_ref[...] = x + acc
    st_ref[0] = carry_ref[...]


def _mixer_a_small_kernel(x_ref, g_ref, wc_ref, wb_ref, wv_ref, wz_ref, cw_ref, wout_ref,
                          p1_ref, p2_ref, hp_ref, u_ref, *, n_meta):
    x = x_ref[...]
    xn = _rms_rows(x, g_ref[...]).astype(BF16)
    u = _dot(xn, wc_ref[...]) * _dot(xn, wv_ref[...])
    row = lax.broadcasted_iota(jnp.int32, u.shape, 0)
    um1 = jnp.where(row == 0, 0.0, jnp.where(row < n_meta, pltpu.roll(u, 1, 0), p1_ref[...]))
    um2 = jnp.where(row < 2, 0.0, jnp.where(row < n_meta, pltpu.roll(u, 2, 0), p2_ref[...]))
    u_ref[...] = u
    contrib = _conv_gate_out(xn, u, um1, um2, wb_ref[...], wz_ref[...], cw_ref[...], wout_ref[...])

    @pl.when(pl.program_id(0) == 0)
    def _():
        hp_ref[...] = x + contrib

    @pl.when(pl.program_id(0) != 0)
    def _():
        hp_ref[...] += contrib


def _mixer_a_body(x2d, gain, w_in, conv_w, w_out, init8, *, batch, seq, tm=512, tc=512):
    nt = seq // tm
    vmem = (4 * tm * D_MODEL * 4
            + w_in.size * 2 + w_out.size * 2
            + 10 * tm * tc * 4 + 2 * tm * D_MODEL * 4
            + (8 << 20))
    return pl.pallas_call(
        functools.partial(_mixer_a_body_kernel, tm=tm, tc=tc),
        grid=(batch, nt),
        in_specs=[
            pl.BlockSpec((tm, D_MODEL), lambda b, t: (b * nt + t, 0)),
            _resident((1, D_MODEL)),
            _resident(w_in.shape),
            _resident(conv_w.shape),
            _resident(w_out.shape),
            _resident(init8.shape),
        ],
        out_specs=[
            pl.BlockSpec((tm, D_MODEL), lambda b, t: (b * nt + t, 0)),
            pl.BlockSpec((1, V7X_SUBLANES, D_CONV), lambda b, t: (b, 0, 0)),
        ],
        out_shape=[
            jax.ShapeDtypeStruct((batch * seq, D_MODEL), F32),
            jax.ShapeDtypeStruct((batch, V7X_SUBLANES, D_CONV), F32),
        ],
        scratch_shapes=[pltpu.VMEM((V7X_SUBLANES, D_CONV), F32)],
        compiler_params=_params(("arbitrary", "arbitrary"), vmem),
        name="mixer_a_body",
    )(x2d, gain, w_in, conv_w, w_out, init8)


def _mixer_a_small(x, gain, w_in, conv_w, w_out, p1, p2, *, n_meta, tc=512):
    rows = x.shape[0]
    nj = D_CONV // tc
    w_spec = lambda k: pl.BlockSpec((D_MODEL, tc), lambda j, k=k: (0, k * nj + j))
    vmem = 2 * (4 * D_MODEL * tc * 2 + tc * D_MODEL * 2) + (8 << 20)
    return pl.pallas_call(
        functools.partial(_mixer_a_small_kernel, n_meta=n_meta),
        grid=(nj,),
        in_specs=[
            pl.BlockSpec((rows, D_MODEL), lambda j: (0, 0)),
            pl.BlockSpec((1, D_MODEL), lambda j: (0, 0)),
            w_spec(0), w_spec(1), w_spec(2), w_spec(3),
            pl.BlockSpec((CONV_W, tc), lambda j: (0, j)),
            pl.BlockSpec((tc, D_MODEL), lambda j: (j, 0)),
            pl.BlockSpec((rows, tc), lambda j: (0, j)),
            pl.BlockSpec((rows, tc), lambda j: (0, j)),
        ],
        out_specs=[
            pl.BlockSpec((rows, D_MODEL), lambda j: (0, 0)),
            pl.BlockSpec((rows, tc), lambda j: (0, j)),
        ],
        out_shape=[
            jax.ShapeDtypeStruct((rows, D_MODEL), F32),
            jax.ShapeDtypeStruct((rows, D_CONV), F32),
        ],
        compiler_params=_params(("arbitrary",), vmem),
        name="mixer_a_small",
    )(x, gain, w_in, w_in, w_in, w_in, conv_w, w_out, p1, p2)


def _rope_tables(pos, tc):
    lane = lax.broadcasted_iota(jnp.int32, (1, V7X_LANES), 1)
    l64 = lane & (HEAD_QK - 1)
    half = ROT_DIM // 2
    idx = jnp.where(l64 < half, l64, l64 - half).astype(F32)
    inv = jnp.where(l64 < ROT_DIM, jnp.exp(idx * (-2.0 / ROT_DIM * math.log(ROPE_THETA))), 0.0)
    ang = pos * inv
    cos = jnp.cos(ang)
    sin = jnp.sin(ang)
    lo = jnp.where(l64 < half, -sin, 0.0)
    hi = jnp.where((l64 >= half) & (l64 < ROT_DIM), sin, 0.0)
    reps = tc // V7X_LANES
    return tuple(jnp.concatenate([t] * reps, axis=1) for t in (cos, lo, hi))


def _norm_rope(p, gain, bd, tables):
    tc = p.shape[1]
    sq = (p * p).astype(BF16)
    ss = jnp.concatenate(
        [_dot(sq[:, i * V7X_MXU_DIM:(i + 1) * V7X_MXU_DIM], bd) for i in range(tc // V7X_MXU_DIM)],
        axis=1)
    pn = p * lax.rsqrt(ss * (1.0 / HEAD_QK) + EPS) * gain
    cos, lo, hi = tables
    half = ROT_DIM // 2
    return pn * cos + pltpu.roll(pn, tc - half, 1) * lo + pltpu.roll(pn, half, 1) * hi


def _proj_b_kernel(h_ref, pos_ref, g_ref, win_ref, qn_ref, kn_ref, bd_ref, *out_refs,
                   tc, head_major):
    xn = _rms_rows(h_ref[...], g_ref[...]).astype(BF16)
    tables = _rope_tables(pos_ref[...], tc)
    bd = bd_ref[...]
    heads_per_chunk = tc // HEAD_V
    if head_major:
        q_ref, kf_ref, vf_ref, zs_ref, kb_ref, vt_ref = out_refs
    else:
        q_ref, kf_ref, vf_ref, zs_ref = out_refs
    for j in range(QK_COLS // tc):
        lo = j * tc
        q = _norm_rope(_dot(xn, win_ref[:, lo:lo + tc]), qn_ref[:, lo:lo + tc], bd, tables)
        q = q * Q_SCALE
        k = _norm_rope(_dot(xn, win_ref[:, QK_COLS + lo:QK_COLS + lo + tc]),
                       kn_ref[:, lo:lo + tc], bd, tables)
        v = _dot(xn, win_ref[:, 2 * QK_COLS + lo:2 * QK_COLS + lo + tc])
        z = _dot(xn, win_ref[:, 2 * QK_COLS + V_COLS + lo:2 * QK_COLS + V_COLS + lo + tc])
        zs_ref[:, lo:lo + tc] = _silu(z).astype(zs_ref.dtype)
        if head_major:
            hlo = j * heads_per_chunk
            kf_ref[0, :, hlo:hlo + heads_per_chunk, :] = k.reshape(k.shape[0], heads_per_chunk, HEAD_V)
            vf_ref[0, :, hlo:hlo + heads_per_chunk, :] = v.reshape(v.shape[0], heads_per_chunk, HEAD_V)
            for hh in range(heads_per_chunk):
                h = j * heads_per_chunk + hh
                sl = slice(hh * HEAD_V, (hh + 1) * HEAD_V)
                q_ref[0, h, 0] = q[:, sl].T.astype(BF16)
                kb_ref[0, h] = k[:, sl].astype(BF16)
                vt_ref[0, h, 0] = v[:, sl].T.astype(BF16)
        else:
            kf_ref[:, lo:lo + tc] = k
            vf_ref[:, lo:lo + tc] = v
            q_ref[:, lo:lo + tc] = q


def _proj_b(h2d, pos, gain, w_in, qn, kn, bd, *, batch, seq, tm, head_major, row_offset=0, tc=512):
    nt = seq // tm
    rows = batch * seq
    row_spec = lambda cols: pl.BlockSpec((tm, cols), lambda b, t: (b * nt + t, 0))
    hm_spec = pl.BlockSpec((1, N_HEADS, tm, HEAD_V), lambda b, t: (b, 0, t, 0))
    hm_shape = jax.ShapeDtypeStruct((batch, N_HEADS, seq, HEAD_V), BF16)
    tr_spec = pl.BlockSpec((1, N_HEADS, 1, HEAD_V, tm), lambda b, t: (b, 0, t, 0, 0))
    tr_shape = jax.ShapeDtypeStruct((batch, N_HEADS, nt, HEAD_V, tm), BF16)
    row_shape = lambda dt: jax.ShapeDtypeStruct((rows, QK_COLS), dt)
    if head_major:
        fin_spec = pl.BlockSpec(
            (pl.Element(1), pl.Element(tm), pl.Element(N_HEADS), pl.Element(HEAD_V)),
            lambda b, t: (b, row_offset + t * tm, 0, 0))
        fin_shape = jax.ShapeDtypeStruct((batch, row_offset + seq, N_HEADS, HEAD_V), F32)
        out_specs = [tr_spec, fin_spec, fin_spec, row_spec(V_COLS), hm_spec, tr_spec]
        out_shape = [tr_shape, fin_shape, fin_shape, row_shape(BF16), hm_shape, tr_shape]
    else:
        out_specs = [row_spec(QK_COLS), row_spec(QK_COLS), row_spec(V_COLS), row_spec(V_COLS)]
        out_shape = [row_shape(F32), row_shape(F32), row_shape(F32), row_shape(F32)]
    vmem = (2 * tm * D_MODEL * 4 + w_in.size * 2
            + 2 * tm * QK_COLS * (4 + 4 + 4 + 4 + 4)
            + 12 * tm * tc * 4 + (8 << 20))
    return pl.pallas_call(
        functools.partial(_proj_b_kernel, tc=tc, head_major=head_major),
        grid=(batch, nt),
        in_specs=[
            row_spec(D_MODEL),
            pl.BlockSpec((tm, V7X_LANES), lambda b, t: (t, 0)),
            _resident((1, D_MODEL)),
            _resident(w_in.shape),
            _resident((1, QK_COLS)),
            _resident((1, QK_COLS)),
            _resident(bd.shape),
        ],
        out_specs=out_specs,
        out_shape=out_shape,
        compiler_params=_params(("arbitrary", "arbitrary"), vmem),
        name="proj_b_body" if head_major else "proj_b_small",
    )(h2d, pos, gain, w_in, qn, kn, bd)


def _online_update(state, st, vt):
    m, l, acc = state
    m_new = jnp.maximum(m, jnp.max(st, axis=0, keepdims=True))
    alpha = jnp.exp2(m - m_new)
    pt = jnp.exp2(st - m_new)
    return (m_new, alpha * l + jnp.sum(pt, axis=0, keepdims=True),
            alpha * acc + _dot(vt, pt.astype(BF16)))


def _attn_full_blocks(i, qt, k_ref, vt_ref, *, tq):
    drow = lax.broadcasted_iota(jnp.int32, qt.shape, 0)
    zero = jnp.zeros_like(qt)
    qtm = (jnp.where(drow < HEAD_QK, qt, zero), jnp.where(drow >= HEAD_QK, qt, zero))

    def scores(j, rows=None):
        kb = k_ref[0, 0, pl.ds(pl.multiple_of(j * tq, tq), tq), :] if rows is None else rows
        return tuple(_dot(kb, qtm[c]) for c in range(2))

    empty = (jnp.full((1, tq), -jnp.inf, F32), jnp.zeros((1, tq), F32), jnp.zeros((HEAD_V, tq), F32))
    n_odd = i & 1

    def one_block(j, states):
        sts = scores(j)
        vt = vt_ref[0, 0, j]
        return tuple(_online_update(states[c], sts[c], vt) for c in range(2))

    def two_blocks(p, states):
        ja = n_odd + 2 * p
        sa = scores(ja)
        sb = scores(ja + 1)
        va = vt_ref[0, 0, ja]
        vb = vt_ref[0, 0, ja + 1]
        states = tuple(_online_update(states[c], sa[c], va) for c in range(2))
        return tuple(_online_update(states[c], sb[c], vb) for c in range(2))

    states = lax.fori_loop(0, n_odd, one_block, (empty, empty))
    states = lax.fori_loop(0, lax.shift_right_logical(i, 1), two_blocks, states)
    return states, scores


def _attn_last_blocks(i, states, scores, vt_ref, kp, vpt, lam, *, tq):
    causal = (lax.broadcasted_iota(jnp.int32, (tq, tq), 0)
              <= lax.broadcasted_iota(jnp.int32, (tq, tq), 1))
    sts = scores(i)
    sps = scores(None, rows=kp)
    vt = vt_ref[0, 0, i]
    outs = []
    for c in range(2):
        m, l, acc = states[c]
        sd = jnp.where(causal, sts[c], -jnp.inf)
        m_new = jnp.maximum(m, jnp.maximum(jnp.max(sd, axis=0, keepdims=True),
                                           jnp.max(sps[c], axis=0, keepdims=True)))
        alpha = jnp.exp2(m - m_new)
        pd = jnp.exp2(sd - m_new)
        pp = jnp.exp2(sps[c] - m_new)
        l = alpha * l + jnp.sum(pd, axis=0, keepdims=True) + jnp.sum(pp, axis=0, keepdims=True)
        acc = alpha * acc + _dot(vt, pd.astype(BF16)) + _dot(vpt, pp.astype(BF16))
        outs.append(acc * (1.0 / l))
    return (outs[0] - lam * outs[1]).T


def _half_sums(x):
    lane = lax.broadcasted_iota(jnp.int32, x.shape, 1)
    low = lane < HEAD_QK
    s_lo = jnp.sum(jnp.where(low, x, 0.0), axis=1, keepdims=True)
    s_hi = jnp.sum(jnp.where(low, 0.0, x), axis=1, keepdims=True)
    return jnp.where(low, s_lo, s_hi)


def _decode_reset(m_ref, l_ref, acc_ref, accx_ref):
    m_ref[...] = jnp.full(m_ref.shape, -jnp.inf, F32)
    l_ref[...] = jnp.zeros(l_ref.shape, F32)
    acc_ref[...] = jnp.zeros(acc_ref.shape, F32)
    accx_ref[...] = jnp.zeros(accx_ref.shape, F32)


def _decode_pages(q, hs, k_refs, v_refs, m_ref, l_ref, acc_ref, accx_ref, *, page):
    n_pg = len(k_refs)

    def page_scores(k_ref):
        prod = (k_ref[0, 0] * q[None]).astype(BF16).reshape(page * N_HEADS, HEAD_V)
        return _dot(prod, hs).reshape(page, N_HEADS, HEAD_V)

    m = m_ref[...]
    l = l_ref[...]
    acc = acc_ref[...]
    accx = accx_ref[...]
    s_cur = page_scores(k_refs[0])
    for i in range(n_pg):
        s_next = page_scores(k_refs[i + 1]) if i + 1 < n_pg else None
        m_new = jnp.maximum(m, jnp.max(s_cur, axis=0))
        alpha = jnp.exp2(m - m_new)
        l = alpha * l
        acc = alpha * acc
        accx = pltpu.roll(alpha, HEAD_QK, 1) * accx
        for t0 in range(0, page, DECODE_CHUNK):
            p3 = jnp.exp2(s_cur[t0:t0 + DECODE_CHUNK] - m_new[None])
            px3 = pltpu.roll(p3.reshape(DECODE_CHUNK * N_HEADS, HEAD_V), HEAD_QK, 1).reshape(p3.shape)
            v3 = v_refs[i][0, 0, t0:t0 + DECODE_CHUNK]
            l = l + jnp.sum(p3, axis=0)
            acc = acc + jnp.sum(v3 * p3, axis=0)
            accx = accx + jnp.sum(v3 * px3, axis=0)
        m = m_new
        s_cur = s_next
    m_ref[...] = m
    l_ref[...] = l
    acc_ref[...] = acc
    accx_ref[...] = accx


def _decode_finish(q, kn, vn, zs, sub, lam, m_ref, l_ref, acc_ref, accx_ref):
    s_self = _half_sums(q * kn)
    m_old = m_ref[...]
    m_fin = jnp.maximum(m_old, s_self)
    a_fin = jnp.exp2(m_old - m_fin)
    p_self = jnp.exp2(s_self - m_fin)
    inv_l = 1.0 / (a_fin * l_ref[...] + p_self)
    w = (a_fin * acc_ref[...] + p_self * vn) * inv_l
    a_x = pltpu.roll(a_fin, HEAD_QK, 1)
    p_x = pltpu.roll(p_self, HEAD_QK, 1)
    wx = (a_x * accx_ref[...] + p_x * vn) * pltpu.roll(inv_l, HEAD_QK, 1)
    low = lax.broadcasted_iota(jnp.int32, w.shape, 1) < HEAD_QK
    d = jnp.where(low, w, wx) - lam * jnp.where(low, wx, w)
    return _rms_rows(d, sub) * (1.0 - LAM_INIT) * zs


def _attn_fused_kernel(pt_ref, qt_ref, k_ref, vt_ref, kp_ref, vpt_ref, zs_ref, sub_ref,
                       lq1_ref, lk1_ref, lq2_ref, lk2_ref, qd_ref, *refs, tq, page, n_pg, groups):
    del pt_ref
    k_pages = refs[:n_pg]
    v_pages = refs[n_pg:2 * n_pg]
    (kn_ref, vn_ref, zsd_ref, hs_ref, g_ref, gd_ref,
     m_ref, l_ref, acc_ref, accx_ref) = refs[2 * n_pg:]
    i = pl.program_id(2)
    step = (pl.program_id(0) * pl.num_programs(1) + pl.program_id(1)) * pl.num_programs(2) + i
    grp = step & (groups - 1)
    dec_state = (m_ref, l_ref, acc_ref, accx_ref)

    @pl.when(grp == 0)
    def _():
        _decode_reset(*dec_state)

    qt = qt_ref[0, 0, 0]
    states, scores = _attn_full_blocks(i, qt, k_ref, vt_ref, tq=tq)

    qd = qd_ref[0]
    half = n_pg // 2
    _decode_pages(qd, hs_ref[...], k_pages[:half], v_pages[:half], *dec_state, page=page)
    lam = _lambda_full(lq1_ref, lk1_ref, lq2_ref, lk2_ref)
    o = _attn_last_blocks(i, states, scores, vt_ref, kp_ref[0], vpt_ref[0], lam, tq=tq)
    _decode_pages(qd, hs_ref[...], k_pages[half:], v_pages[half:], *dec_state, page=page)
    on = _rms_rows(o, sub_ref[...]) * (1.0 - LAM_INIT)
    g_ref[...] = (on * zs_ref[...].astype(F32)).astype(g_ref.dtype)

    @pl.when(grp == groups - 1)
    def _():
        gd_ref[0] = _decode_finish(qd, kn_ref[0], vn_ref[0], zsd_ref[0], sub_ref[...], lam, *dec_state)


def _attn_fused(page_table, qt, kb, vt, kp, vpt, zs, subln, lams,
                q_s, cache_k, cache_v, k_new, v_new, zs_s, hs, *, batch, seq, tq, n_pg=4):
    nq = seq // tq
    dec_batch, n_pages = page_table.shape
    page = cache_k.shape[2]
    groups = n_pages // n_pg
    assert n_pages % n_pg == 0 and batch * N_HEADS * nq == dec_batch * groups
    assert groups & (groups - 1) == 0

    def dec_pos(b, h, i):
        step = (b * N_HEADS + h) * nq + i
        return lax.shift_right_logical(step, groups.bit_length() - 1), step & (groups - 1)

    per_row = pl.BlockSpec((1, N_HEADS, HEAD_V), lambda b, h, i, pt: (dec_pos(b, h, i)[0], 0, 0))

    def page_spec(j):
        def index(b, h, i, pt):
            row, grp = dec_pos(b, h, i)
            return (0, pt[row, grp * n_pg + j], 0, 0, 0)
        return pl.BlockSpec((1, 1, page, N_HEADS, HEAD_V), index)

    page_specs = [page_spec(j) for j in range(n_pg)]
    tr_head = pl.BlockSpec((1, 1, nq, HEAD_V, tq), lambda b, h, i, pt: (b, h, 0, 0, 0))
    vec = pl.BlockSpec((1, HEAD_QK), lambda b, h, i, pt: (0, 0))
    tile = pl.BlockSpec((tq, HEAD_V), lambda b, h, i, pt: (b * nq + i, h))
    acc = pltpu.VMEM((N_HEADS, HEAD_V), F32)
    page_bytes = page * N_HEADS * HEAD_V * 4
    vmem = (n_pg * (4 * page_bytes + 5 * page_bytes)
            + 4 * seq * HEAD_V * 2 + 10 * tq * tq * 4 + (8 << 20))
    return pl.pallas_call(
        functools.partial(_attn_fused_kernel, tq=tq, page=page, n_pg=n_pg, groups=groups),
        grid_spec=pltpu.PrefetchScalarGridSpec(
            num_scalar_prefetch=1,
            grid=(batch, N_HEADS, nq),
            in_specs=[
                pl.BlockSpec((1, 1, 1, HEAD_V, tq), lambda b, h, i, pt: (b, h, i, 0, 0)),
                pl.BlockSpec((1, 1, seq, HEAD_V), lambda b, h, i, pt: (b, h, 0, 0)),
                tr_head,
                pl.BlockSpec((1, N_META, HEAD_V), lambda b, h, i, pt: (h, 0, 0)),
                pl.BlockSpec((1, HEAD_V, N_META), lambda b, h, i, pt: (h, 0, 0)),
                tile,
                pl.BlockSpec((1, HEAD_V), lambda b, h, i, pt: (0, 0)),
                vec, vec, vec, vec,
                per_row,
            ] + page_specs + page_specs + [
                per_row, per_row, per_row,
                pl.BlockSpec(hs.shape, lambda b, h, i, pt: (0, 0)),
            ],
            out_specs=[tile, per_row],
            scratch_shapes=[acc, acc, acc, acc],
        ),
        out_shape=[jax.ShapeDtypeStruct((batch * seq, V_COLS), BF16),
                   jax.ShapeDtypeStruct((dec_batch, N_HEADS, HEAD_V), F32)],
        compiler_params=_params(("arbitrary", "arbitrary", "arbitrary"), vmem),
        name="attn_fused",
    )(page_table, qt, kb, vt, kp, vpt, zs, subln, *lams,
      q_s, *([cache_k] * n_pg), *([cache_v] * n_pg), k_new, v_new, zs_s, hs)


def _out_b_kernel(g_ref, h_ref, w_ref, y_ref):
    y_ref[...] = h_ref[...] + _dot(g_ref[...].astype(BF16), w_ref[...])


def _out_b(g, h, w_out, *, tm):
    rows = g.shape[0]
    vmem = 2 * tm * (V_COLS * 4 + 2 * D_MODEL * 4) + w_out.size * 2 + 2 * tm * D_MODEL * 4 + (8 << 20)
    return pl.pallas_call(
        _out_b_kernel,
        grid=(rows // tm,),
        in_specs=[
            pl.BlockSpec((tm, V_COLS), lambda t: (t, 0)),
            pl.BlockSpec((tm, D_MODEL), lambda t: (t, 0)),
            _resident(w_out.shape),
        ],
        out_specs=pl.BlockSpec((tm, D_MODEL), lambda t: (t, 0)),
        out_shape=jax.ShapeDtypeStruct((rows, D_MODEL), F32),
        compiler_params=_params(("arbitrary",), vmem),
        name="out_b",
    )(g, h, w_out)


def _fill_leading_rows_kernel(rows_ref, full_ref, out_ref):
    del full_ref
    out_ref[0] = rows_ref[...]


def _fill_leading_rows(full, rows):
    batch = full.shape[0]
    return pl.pallas_call(
        _fill_leading_rows_kernel,
        grid=(batch,),
        in_specs=[pl.BlockSpec(rows.shape, lambda b: (0, 0, 0)),
                  pl.BlockSpec(memory_space=pl.ANY)],
        out_specs=pl.BlockSpec((1,) + rows.shape, lambda b: (b, 0, 0, 0)),
        out_shape=jax.ShapeDtypeStruct(full.shape, full.dtype),
        input_output_aliases={1: 0},
        name="fill_leading_rows",
    )(rows, full)


def kernel(x_prompt, x_sample, state_conv, cache_k, cache_v, page_table, meta_tokens, norm_a, w_in_a, conv_w_a, w_out_a, norm_b, w_in_b, q_norm_b, k_norm_b, lambda_q1_b, lambda_k1_b, lambda_q2_b, lambda_k2_b, subln_b, w_out_b):
    batch, seq, _ = x_prompt.shape
    dec_batch, dec_seq, _ = x_sample.shape
    n_meta = meta_tokens.shape[0]
    assert norm_a.shape[0] == 1 and norm_b.shape[0] == 1 and dec_seq == 1 and n_meta == N_META
    n_pages = page_table.shape[1]
    page = cache_k.shape[2]
    past_len = n_pages * page

    w_in_a_bf = w_in_a[0].astype(BF16)
    w_out_a_bf = w_out_a[0].astype(BF16)
    w_in_b_bf = w_in_b[0].astype(BF16)
    w_out_b_bf = w_out_b[0].astype(BF16)
    gain_a = norm_a[0][None, :]
    gain_b = norm_b[0][None, :]

    x_small = jnp.concatenate([meta_tokens, x_sample[:, 0, :]], axis=0)
    zpad = jnp.zeros((n_meta, D_CONV), F32)
    p1 = jnp.concatenate([zpad, state_conv[0, :, 1, :]], axis=0)
    p2 = jnp.concatenate([zpad, state_conv[0, :, 0, :]], axis=0)
    h_small, u_small = _mixer_a_small(x_small, gain_a, w_in_a_bf, conv_w_a[0], w_out_a_bf,
                                      p1, p2, n_meta=n_meta)
    init8 = u_small[n_meta - V7X_SUBLANES:n_meta]
    h_body, st_body = _mixer_a_body(x_prompt.reshape(batch * seq, D_MODEL), gain_a, w_in_a_bf,
                                    conv_w_a[0], w_out_a_bf, init8, batch=batch, seq=seq)
    conv_prompt = st_body[:, V7X_SUBLANES - (CONV_W - 1):, :][None]
    conv_sample = jnp.stack([state_conv[0, :, 1, :], u_small[n_meta:]], axis=1)[None]

    qn_t = jnp.tile(q_norm_b[0], N_MAPS)[None, :]
    kn_t = jnp.tile(k_norm_b[0], N_MAPS)[None, :]
    gi = np.arange(V7X_MXU_DIM) // HEAD_QK
    bd = jnp.asarray(gi[:, None] == gi[None, :], dtype=BF16)
    lanes1 = jnp.ones((1, V7X_LANES), F32)
    pos_body = (n_meta + jnp.arange(seq, dtype=F32))[:, None] * lanes1
    pos_small = jnp.concatenate([jnp.arange(n_meta, dtype=F32),
                                 jnp.full((dec_batch,), float(past_len), F32)])[:, None] * lanes1
    q_s, k_small, v_small, zs_small = _proj_b(
        h_small, pos_small, gain_b, w_in_b_bf, qn_t, kn_t, bd,
        batch=1, seq=n_meta + dec_batch, tm=n_meta + dec_batch, head_major=False)
    qt_hm, k_body, v_body, zs_body, k_hm, vt_hm = _proj_b(
        h_body, pos_body, gain_b, w_in_b_bf, qn_t, kn_t, bd,
        batch=batch, seq=seq, tm=ATTN_TILE, head_major=True, row_offset=n_meta)

    lams = tuple(a[0][None, :] for a in (lambda_q1_b, lambda_k1_b, lambda_q2_b, lambda_k2_b))

    meta_heads = lambda a: a[:n_meta].reshape(n_meta, N_HEADS, HEAD_V).astype(BF16)
    per_head = lambda a: a[n_meta:].reshape(dec_batch, N_HEADS, HEAD_V)
    li = np.arange(HEAD_V) // HEAD_QK
    half_sum = jnp.asarray(li[:, None] == li[None, :], dtype=BF16)
    g_body, g_dec = _attn_fused(
        page_table, qt_hm, k_hm, vt_hm, meta_heads(k_small).transpose(1, 0, 2),
        meta_heads(v_small).transpose(1, 2, 0), zs_body, subln_b[0][None, :], lams,
        per_head(q_s), cache_k, cache_v, per_head(k_small), per_head(v_small), per_head(zs_small),
        half_sum, batch=batch, seq=seq, tq=ATTN_TILE)
    y_body = _out_b(g_body, h_body, w_out_b_bf, tm=512)
    y_dec = _out_b(g_dec.reshape(dec_batch, V_COLS), h_small[n_meta:], w_out_b_bf, tm=dec_batch)

    def with_meta(small, body):
        return _fill_leading_rows(body, small[:n_meta].reshape(n_meta, N_HEADS, HEAD_V))[None]

    return (
        y_body.reshape(batch, seq, D_MODEL),
        y_dec.reshape(dec_batch, 1, D_MODEL),
        conv_prompt,
        conv_sample,
        with_meta(k_small, k_body),
        with_meta(v_small, v_body),
        k_small[n_meta:].reshape(1, dec_batch, 1, N_HEADS, HEAD_V),
        v_small[n_meta:].reshape(1, dec_batch, 1, N_HEADS, HEAD_V),
    )
```

```python
import functools
import math

import numpy as np
import jax
import jax.numpy as jnp
from jax import lax
from jax.experimental import pallas as pl
from jax.experimental.pallas import tpu as pltpu

D_MODEL = 1024
D_CONV = 2048
CONV_W = 3
N_HEADS = 16
HEAD_QK = 64
HEAD_V = 128
ROT_DIM = 16
ROPE_THETA = 500000.0
EPS = 1e-6
N_META = 16
QK_COLS = N_HEADS * 2 * HEAD_QK
V_COLS = N_HEADS * HEAD_V
N_MAPS = 2 * N_HEADS
LAM_INIT = 0.8 - 0.6 * math.exp(-0.3 * 1)
SCORE_SCALE = HEAD_QK ** -0.5
Q_SCALE = SCORE_SCALE * math.log2(math.e)
LOG2_HEAD_QK = HEAD_QK.bit_length() - 1
LOG2_HEAD_V = HEAD_V.bit_length() - 1
LOG2_N_HEADS = N_HEADS.bit_length() - 1

V7X_LANES = 128
V7X_SUBLANES = 8
V7X_MXU_DIM = 256
V7X_VMEM_BYTES = 64 * 1024 * 1024

ATTN_TILE = 512
DECODE_CHUNK = 16

F32 = jnp.float32
BF16 = jnp.bfloat16


def _dot(a, b):
    return jnp.dot(a, b, preferred_element_type=F32)


def _rms_rows(x, gain):
    ms = jnp.mean(x * x, axis=-1, keepdims=True)
    return x * lax.rsqrt(ms + EPS) * gain


def _silu(z):
    return z * jax.nn.sigmoid(z)


def _lambda_full(lam_ref):
    a = jnp.sum(lam_ref[0:1, :] * lam_ref[1:2, :], axis=-1, keepdims=True)
    b = jnp.sum(lam_ref[2:3, :] * lam_ref[3:4, :], axis=-1, keepdims=True)
    return jnp.exp(a) - jnp.exp(b) + LAM_INIT


def _params(semantics, vmem_bytes):
    return pltpu.CompilerParams(dimension_semantics=semantics,
                                vmem_limit_bytes=min(int(vmem_bytes), V7X_VMEM_BYTES - (4 << 20)))


def _resident(shape):
    return pl.BlockSpec(shape, lambda *_: (0,) * len(shape), pipeline_mode=pl.Buffered(1))


def _conv_gate_out(xn, u, um1, um2, wb, wz, cw, wout):
    conv = cw[0:1, :] * um2 + cw[1:2, :] * um1 + cw[2:3, :] * u
    y = _dot(xn, wb) * conv * _silu(_dot(xn, wz))
    return _dot(y.astype(BF16), wout)


def _mixer_a_body_kernel(x_ref, g_ref, win_ref, cw_ref, wout_ref, init_ref,
                         hp_ref, st_ref, carry_ref, *, tm, tc):
    @pl.when(pl.program_id(1) == 0)
    def _():
        carry_ref[...] = init_ref[...]

    x = x_ref[...]
    xn = _rms_rows(x, g_ref[...]).astype(BF16)
    row = lax.broadcasted_iota(jnp.int32, (tm, tc), 0)
    acc = jnp.zeros((tm, D_MODEL), F32)
    for j in range(D_CONV // tc):
        lo = j * tc
        u = _dot(xn, win_ref[:, lo:lo + tc]) * _dot(xn, win_ref[:, 2 * D_CONV + lo:2 * D_CONV + lo + tc])
        prev = carry_ref[:, lo:lo + tc]
        p1 = prev[7:8, :]
        p2 = prev[6:7, :]
        um1 = jnp.where(row == 0, p1, pltpu.roll(u, 1, 0))
        um2 = jnp.where(row == 0, p2, jnp.where(row == 1, p1, pltpu.roll(u, 2, 0)))
        carry_ref[:, lo:lo + tc] = u[tm - V7X_SUBLANES:tm, :]
        acc = acc + _conv_gate_out(
            xn, u, um1, um2,
            win_ref[:, D_CONV + lo:D_CONV + lo + tc],
            win_ref[:, 3 * D_CONV + lo:3 * D_CONV + lo + tc],
            cw_ref[:, lo:lo + tc], wout_ref[lo:lo + tc, :])
    hp_ref[...] = x + acc
    st_ref[0] = carry_ref[...]


def _mixer_a_small_kernel(x_ref, g_ref, wc_ref, wb_ref, wv_ref, wz_ref, cw_ref, wout_ref,
                          p1_ref, p2_ref, hp_ref, u_ref, *, n_meta):
    x = x_ref[...]
    xn = _rms_rows(x, g_ref[...]).astype(BF16)
    u = _dot(xn, wc_ref[...]) * _dot(xn, wv_ref[...])
    row = lax.broadcasted_iota(jnp.int32, u.shape, 0)
    um1 = jnp.where(row == 0, 0.0, jnp.where(row < n_meta, pltpu.roll(u, 1, 0), p1_ref[...]))
    um2 = jnp.where(row < 2, 0.0, jnp.where(row < n_meta, pltpu.roll(u, 2, 0), p2_ref[...]))
    u_ref[...] = u
    contrib = _conv_gate_out(xn, u, um1, um2, wb_ref[...], wz_ref[...], cw_ref[...], wout_ref[...])

    @pl.when(pl.program_id(0) == 0)
    def _():
        hp_ref[...] = x + contrib

    @pl.when(pl.program_id(0) != 0)
    def _():
        hp_ref[...] += contrib


def _mixer_a_body(x2d, gain, w_in, conv_w, w_out, init8, *, batch, seq, tm=1024, tc=256):
    nt = seq // tm
    vmem = (4 * tm * D_MODEL * 4
            + w_in.size * 2 + w_out.size * 2
            + 10 * tm * tc * 4 + 2 * tm * D_MODEL * 4
            + (8 << 20))
    return pl.pallas_call(
        functools.partial(_mixer_a_body_kernel, tm=tm, tc=tc),
        grid=(batch, nt),
        in_specs=[
            pl.BlockSpec((tm, D_MODEL), lambda b, t: (b * nt + t, 0)),
            _resident((1, D_MODEL)),
            _resident(w_in.shape),
            _resident(conv_w.shape),
            _resident(w_out.shape),
            _resident(init8.shape),
        ],
        out_specs=[
            pl.BlockSpec((tm, D_MODEL), lambda b, t: (b * nt + t, 0)),
            pl.BlockSpec((1, V7X_SUBLANES, D_CONV), lambda b, t: (b, 0, 0)),
        ],
        out_shape=[
            jax.ShapeDtypeStruct((batch * seq, D_MODEL), F32),
            jax.ShapeDtypeStruct((batch, V7X_SUBLANES, D_CONV), F32),
        ],
        scratch_shapes=[pltpu.VMEM((V7X_SUBLANES, D_CONV), F32)],
        compiler_params=_params(("arbitrary", "arbitrary"), vmem),
        name="mixer_a_body",
    )(x2d, gain, w_in, conv_w, w_out, init8)


def _mixer_a_small(x, gain, w_in, conv_w, w_out, p1, p2, *, n_meta, tc=512):
    rows = x.shape[0]
    nj = D_CONV // tc
    w_spec = lambda k: pl.BlockSpec((D_MODEL, tc), lambda j, k=k: (0, k * nj + j))
    vmem = 2 * (4 * D_MODEL * tc * 2 + tc * D_MODEL * 2) + (8 << 20)
    return pl.pallas_call(
        functools.partial(_mixer_a_small_kernel, n_meta=n_meta),
        grid=(nj,),
        in_specs=[
            pl.BlockSpec((rows, D_MODEL), lambda j: (0, 0)),
            pl.BlockSpec((1, D_MODEL), lambda j: (0, 0)),
            w_spec(0), w_spec(1), w_spec(2), w_spec(3),
            pl.BlockSpec((CONV_W, tc), lambda j: (0, j)),
            pl.BlockSpec((tc, D_MODEL), lambda j: (j, 0)),
            pl.BlockSpec((rows, tc), lambda j: (0, j)),
            pl.BlockSpec((rows, tc), lambda j: (0, j)),
        ],
        out_specs=[
            pl.BlockSpec((rows, D_MODEL), lambda j: (0, 0)),
            pl.BlockSpec((rows, tc), lambda j: (0, j)),
        ],
        out_shape=[
            jax.ShapeDtypeStruct((rows, D_MODEL), F32),
            jax.ShapeDtypeStruct((rows, D_CONV), F32),
        ],
        compiler_params=_params(("arbitrary",), vmem),
        name="mixer_a_small",
    )(x, gain, w_in, w_in, w_in, w_in, conv_w, w_out, p1, p2)


def _rope_tables(pos, tc):
    lane = lax.broadcasted_iota(jnp.int32, (1, V7X_LANES), 1)
    l64 = lane & (HEAD_QK - 1)
    half = ROT_DIM // 2
    idx = jnp.where(l64 < half, l64, l64 - half).astype(F32)
    inv = jnp.where(l64 < ROT_DIM, jnp.exp(idx * (-2.0 / ROT_DIM * math.log(ROPE_THETA))), 0.0)
    ang = pos * inv
    cos = jnp.cos(ang)
    sin = jnp.sin(ang)
    lo = jnp.where(l64 < half, -sin, 0.0)
    hi = jnp.where((l64 >= half) & (l64 < ROT_DIM), sin, 0.0)
    reps = tc // V7X_LANES
    return tuple(jnp.concatenate([t] * reps, axis=1) for t in (cos, lo, hi))


def _norm_rope(p, gain, bd, tables):
    tc = p.shape[1]
    sq = (p * p).astype(BF16)
    ss = jnp.concatenate(
        [_dot(sq[:, i * V7X_MXU_DIM:(i + 1) * V7X_MXU_DIM], bd) for i in range(tc // V7X_MXU_DIM)],
        axis=1)
    pn = p * lax.rsqrt(ss * (1.0 / HEAD_QK) + EPS) * gain
    cos, lo, hi = tables
    half = ROT_DIM // 2
    return pn * cos + pltpu.roll(pn, tc - half, 1) * lo + pltpu.roll(pn, half, 1) * hi


def _proj_b_kernel(h_ref, pos_ref, g_ref, win_ref, qn_ref, kn_ref, bd_ref, *out_refs,
                   tc, head_major):
    xn = _rms_rows(h_ref[...], g_ref[...]).astype(BF16)
    tables = _rope_tables(pos_ref[...], tc)
    bd = bd_ref[...]
    heads_per_chunk = tc // HEAD_V
    if head_major:
        q_ref, kf_ref, vf_ref, zs_ref, kb_ref, vt_ref = out_refs
    else:
        q_ref, kf_ref, vf_ref, zs_ref = out_refs
    for j in range(QK_COLS // tc):
        lo = j * tc
        q = _norm_rope(_dot(xn, win_ref[:, lo:lo + tc]), qn_ref[:, lo:lo + tc], bd, tables)
        q = q * Q_SCALE
        k = _norm_rope(_dot(xn, win_ref[:, QK_COLS + lo:QK_COLS + lo + tc]),
                       kn_ref[:, lo:lo + tc], bd, tables)
        v = _dot(xn, win_ref[:, 2 * QK_COLS + lo:2 * QK_COLS + lo + tc])
        z = _dot(xn, win_ref[:, 2 * QK_COLS + V_COLS + lo:2 * QK_COLS + V_COLS + lo + tc])
        zs_ref[:, lo:lo + tc] = _silu(z).astype(zs_ref.dtype)
        if head_major:
            hlo = j * heads_per_chunk
            kf_ref[0, :, hlo:hlo + heads_per_chunk, :] = k.reshape(k.shape[0], heads_per_chunk, HEAD_V)
            vf_ref[0, :, hlo:hlo + heads_per_chunk, :] = v.reshape(v.shape[0], heads_per_chunk, HEAD_V)
            for hh in range(heads_per_chunk):
                h = j * heads_per_chunk + hh
                sl = slice(hh * HEAD_V, (hh + 1) * HEAD_V)
                q_ref[0, h, 0] = q[:, sl].T.astype(BF16)
                kb_ref[0, h] = k[:, sl].astype(BF16)
                vt_ref[0, h, 0] = v[:, sl].T.astype(BF16)
        else:
            kf_ref[:, lo:lo + tc] = k
            vf_ref[:, lo:lo + tc] = v
            q_ref[:, lo:lo + tc] = q


def _proj_b(h2d, pos, gain, w_in, qn, kn, bd, *, batch, seq, tm, head_major, row_offset=0, tc=512):
    nt = seq // tm
    rows = batch * seq
    row_spec = lambda cols: pl.BlockSpec((tm, cols), lambda b, t: (b * nt + t, 0))
    hm_spec = pl.BlockSpec((1, N_HEADS, tm, HEAD_V), lambda b, t: (b, 0, t, 0))
    hm_shape = jax.ShapeDtypeStruct((batch, N_HEADS, seq, HEAD_V), BF16)
    tr_spec = pl.BlockSpec((1, N_HEADS, 1, HEAD_V, tm), lambda b, t: (b, 0, t, 0, 0))
    tr_shape = jax.ShapeDtypeStruct((batch, N_HEADS, nt, HEAD_V, tm), BF16)
    row_shape = lambda dt: jax.ShapeDtypeStruct((rows, QK_COLS), dt)
    if head_major:
        fin_spec = pl.BlockSpec(
            (pl.Element(1), pl.Element(tm), pl.Element(N_HEADS), pl.Element(HEAD_V)),
            lambda b, t: (b, row_offset + t * tm, 0, 0))
        fin_shape = jax.ShapeDtypeStruct((batch, row_offset + seq, N_HEADS, HEAD_V), F32)
        out_specs = [tr_spec, fin_spec, fin_spec, row_spec(V_COLS), hm_spec, tr_spec]
        out_shape = [tr_shape, fin_shape, fin_shape, row_shape(BF16), hm_shape, tr_shape]
    else:
        out_specs = [row_spec(QK_COLS), row_spec(QK_COLS), row_spec(V_COLS), row_spec(V_COLS)]
        out_shape = [row_shape(F32), row_shape(F32), row_shape(F32), row_shape(F32)]
    vmem = (2 * tm * D_MODEL * 4 + w_in.size * 2
            + 2 * tm * QK_COLS * (4 + 4 + 4 + 4 + 4)
            + 12 * tm * tc * 4 + (8 << 20))
    return pl.pallas_call(
        functools.partial(_proj_b_kernel, tc=tc, head_major=head_major),
        grid=(batch, nt),
        in_specs=[
            row_spec(D_MODEL),
            pl.BlockSpec((tm, V7X_LANES), lambda b, t: (t, 0)),
            _resident((1, D_MODEL)),
            _resident(w_in.shape),
            _resident((1, QK_COLS)),
            _resident((1, QK_COLS)),
            _resident(bd.shape),
        ],
        out_specs=out_specs,
        out_shape=out_shape,
        compiler_params=_params(("arbitrary", "arbitrary"), vmem),
        name="proj_b_body" if head_major else "proj_b_small",
    )(h2d, pos, gain, w_in, qn, kn, bd)


def _online_update(state, st, vt):
    m, l, acc = state
    m_new = jnp.maximum(m, jnp.max(st, axis=0, keepdims=True))
    alpha = jnp.exp2(m - m_new)
    pt = jnp.exp2(st - m_new)
    return (m_new, alpha * l + jnp.sum(pt, axis=0, keepdims=True),
            alpha * acc + _dot(vt, pt.astype(BF16)))


def _attn_full_blocks(i, qt, k_ref, vt_ref, *, tq):
    drow = lax.broadcasted_iota(jnp.int32, qt.shape, 0)
    zero = jnp.zeros_like(qt)
    qtm = (jnp.where(drow < HEAD_QK, qt, zero), jnp.where(drow >= HEAD_QK, qt, zero))

    def scores(j):
        kb = k_ref[0, 0, pl.ds(pl.multiple_of(j * tq, tq), tq), :]
        return tuple(_dot(kb, qtm[c]) for c in range(2))

    empty = (jnp.full((1, tq), -jnp.inf, F32), jnp.zeros((1, tq), F32), jnp.zeros((HEAD_V, tq), F32))
    n_odd = i & 1

    def one_block(j, states):
        sts = scores(j)
        vt = vt_ref[0, 0, j]
        return tuple(_online_update(states[c], sts[c], vt) for c in range(2))

    def two_blocks(p, states):
        ja = n_odd + 2 * p
        sa = scores(ja)
        sb = scores(ja + 1)
        va = vt_ref[0, 0, ja]
        vb = vt_ref[0, 0, ja + 1]
        states = tuple(_online_update(states[c], sa[c], va) for c in range(2))
        return tuple(_online_update(states[c], sb[c], vb) for c in range(2))

    states = lax.fori_loop(0, n_odd, one_block, (empty, empty))
    states = lax.fori_loop(0, lax.shift_right_logical(i, 1), two_blocks, states)
    return states, qtm


def _attn_last_blocks(i, states, qtm, k_ref, vt_ref, kp, vpt, lam, *, tq):
    causal = (lax.broadcasted_iota(jnp.int32, (tq, tq), 0)
              <= lax.broadcasted_iota(jnp.int32, (tq, tq), 1))
    kd = k_ref[0, 0, pl.ds(pl.multiple_of(i * tq, tq), tq), :]
    sts = tuple(_dot(kd, qtm[c]) for c in range(2))
    sps = tuple(_dot(kp, qtm[c]) for c in range(2))
    vt = vt_ref[0, 0, i]
    outs = []
    for c in range(2):
        m, l, acc = states[c]
        sd = jnp.where(causal, sts[c], -jnp.inf)
        m_new = jnp.maximum(m, jnp.maximum(jnp.max(sd, axis=0, keepdims=True),
                                           jnp.max(sps[c], axis=0, keepdims=True)))
        alpha = jnp.exp2(m - m_new)
        pd = jnp.exp2(sd - m_new)
        pp = jnp.exp2(sps[c] - m_new)
        l = alpha * l + jnp.sum(pd, axis=0, keepdims=True) + jnp.sum(pp, axis=0, keepdims=True)
        acc = alpha * acc + _dot(vt, pd.astype(BF16)) + _dot(vpt, pp.astype(BF16))
        outs.append(acc * (1.0 / l))
    return (outs[0] - lam * outs[1]).T


def _half_sums(x):
    lane = lax.broadcasted_iota(jnp.int32, x.shape, 1)
    low = lane < HEAD_QK
    s_lo = jnp.sum(jnp.where(low, x, 0.0), axis=1, keepdims=True)
    s_hi = jnp.sum(jnp.where(low, 0.0, x), axis=1, keepdims=True)
    return jnp.where(low, s_lo, s_hi)


def _decode_reset(m_ref, l_ref, acc_ref, accx_ref):
    m_ref[...] = jnp.full(m_ref.shape, -jnp.inf, F32)
    l_ref[...] = jnp.zeros(l_ref.shape, F32)
    acc_ref[...] = jnp.zeros(acc_ref.shape, F32)
    accx_ref[...] = jnp.zeros(accx_ref.shape, F32)


def _decode_pages(q, hs, k_refs, v_refs, m_ref, l_ref, acc_ref, accx_ref, *, page):
    n_pg = len(k_refs)

    def page_scores(k_ref):
        prod = (k_ref[0, 0] * q[None]).astype(BF16).reshape(page * N_HEADS, HEAD_V)
        return _dot(prod, hs).reshape(page, N_HEADS, HEAD_V)

    m = m_ref[...]
    l = l_ref[...]
    acc = acc_ref[...]
    accx = accx_ref[...]
    s_cur = page_scores(k_refs[0])
    for i in range(n_pg):
        s_next = page_scores(k_refs[i + 1]) if i + 1 < n_pg else None
        m_new = jnp.maximum(m, jnp.max(s_cur, axis=0))
        alpha = jnp.exp2(m - m_new)
        l = alpha * l
        acc = alpha * acc
        accx = pltpu.roll(alpha, HEAD_QK, 1) * accx
        for t0 in range(0, page, DECODE_CHUNK):
            p3 = jnp.exp2(s_cur[t0:t0 + DECODE_CHUNK] - m_new[None])
            px3 = pltpu.roll(p3.reshape(DECODE_CHUNK * N_HEADS, HEAD_V), HEAD_QK, 1).reshape(p3.shape)
            v3 = v_refs[i][0, 0, t0:t0 + DECODE_CHUNK]
            l = l + jnp.sum(p3, axis=0)
            acc = acc + jnp.sum(v3 * p3, axis=0)
            accx = accx + jnp.sum(v3 * px3, axis=0)
        m = m_new
        s_cur = s_next
    m_ref[...] = m
    l_ref[...] = l
    acc_ref[...] = acc
    accx_ref[...] = accx


def _decode_finish(q, kn, vn, zs, sub, lam, m_ref, l_ref, acc_ref, accx_ref):
    s_self = _half_sums(q * kn)
    m_old = m_ref[...]
    m_fin = jnp.maximum(m_old, s_self)
    a_fin = jnp.exp2(m_old - m_fin)
    p_self = jnp.exp2(s_self - m_fin)
    inv_l = 1.0 / (a_fin * l_ref[...] + p_self)
    w = (a_fin * acc_ref[...] + p_self * vn) * inv_l
    a_x = pltpu.roll(a_fin, HEAD_QK, 1)
    p_x = pltpu.roll(p_self, HEAD_QK, 1)
    wx = (a_x * accx_ref[...] + p_x * vn) * pltpu.roll(inv_l, HEAD_QK, 1)
    low = lax.broadcasted_iota(jnp.int32, w.shape, 1) < HEAD_QK
    d = jnp.where(low, w, wx) - lam * jnp.where(low, wx, w)
    return _rms_rows(d, sub) * (1.0 - LAM_INIT) * zs


def _attn_fused_kernel(pt_ref, qt_ref, k_ref, vt_ref, kp_ref, vpt_ref, zs_ref, sub_ref,
                       lam_ref, dec_ref, *refs, tq, page, n_pg, groups):
    del pt_ref
    k_pages = refs[:n_pg]
    v_pages = refs[n_pg:2 * n_pg]
    hs_ref, g_ref, gd_ref, m_ref, l_ref, acc_ref, accx_ref = refs[2 * n_pg:]
    i = pl.program_id(2)
    step = (pl.program_id(0) * pl.num_programs(1) + pl.program_id(1)) * pl.num_programs(2) + i
    grp = step & (groups - 1)
    dec_state = (m_ref, l_ref, acc_ref, accx_ref)

    @pl.when(grp == 0)
    def _():
        _decode_reset(*dec_state)

    qt = qt_ref[0, 0, 0]
    states, qtm = _attn_full_blocks(i, qt, k_ref, vt_ref, tq=tq)

    qd = dec_ref[0, 0]
    half = n_pg // 2
    _decode_pages(qd, hs_ref[...], k_pages[:half], v_pages[:half], *dec_state, page=page)
    lam = _lambda_full(lam_ref)
    o = _attn_last_blocks(i, states, qtm, k_ref, vt_ref, kp_ref[0], vpt_ref[0], lam, tq=tq)
    _decode_pages(qd, hs_ref[...], k_pages[half:], v_pages[half:], *dec_state, page=page)
    on = _rms_rows(o, sub_ref[...]) * (1.0 - LAM_INIT)
    g_ref[...] = (on * zs_ref[...].astype(F32)).astype(g_ref.dtype)

    @pl.when(grp == groups - 1)
    def _():
        gd_ref[0] = _decode_finish(qd, dec_ref[0, 1], dec_ref[0, 2], dec_ref[0, 3], sub_ref[...],
                                   lam, *dec_state)


def _attn_fused(page_table, qt, kb, vt, kp, vpt, zs, subln, lams,
                dec_rows, cache_k, cache_v, hs, *, batch, seq, tq, n_pg=4):
    nq = seq // tq
    dec_batch, n_pages = page_table.shape
    page = cache_k.shape[2]
    groups = n_pages // n_pg
    assert n_pages % n_pg == 0 and batch * N_HEADS * nq == dec_batch * groups
    assert groups & (groups - 1) == 0

    def dec_pos(b, h, i):
        step = (b * N_HEADS + h) * nq + i
        return lax.shift_right_logical(step, groups.bit_length() - 1), step & (groups - 1)

    per_row = pl.BlockSpec((1, N_HEADS, HEAD_V), lambda b, h, i, pt: (dec_pos(b, h, i)[0], 0, 0))

    def page_spec(j):
        def index(b, h, i, pt):
            row, grp = dec_pos(b, h, i)
            return (0, pt[row, grp * n_pg + j], 0, 0, 0)
        return pl.BlockSpec((1, 1, page, N_HEADS, HEAD_V), index)

    page_specs = [page_spec(j) for j in range(n_pg)]
    tr_head = pl.BlockSpec((1, 1, nq, HEAD_V, tq), lambda b, h, i, pt: (b, h, 0, 0, 0))
    tile = pl.BlockSpec((tq, HEAD_V), lambda b, h, i, pt: (b * nq + i, h))
    acc = pltpu.VMEM((N_HEADS, HEAD_V), F32)
    page_bytes = page * N_HEADS * HEAD_V * 4
    vmem = (n_pg * (4 * page_bytes + 5 * page_bytes)
            + 4 * seq * HEAD_V * 2 + 10 * tq * tq * 4 + (8 << 20))
    return pl.pallas_call(
        functools.partial(_attn_fused_kernel, tq=tq, page=page, n_pg=n_pg, groups=groups),
        grid_spec=pltpu.PrefetchScalarGridSpec(
            num_scalar_prefetch=1,
            grid=(batch, N_HEADS, nq),
            in_specs=[
                pl.BlockSpec((1, 1, 1, HEAD_V, tq), lambda b, h, i, pt: (b, h, i, 0, 0)),
                pl.BlockSpec((1, 1, seq, HEAD_V), lambda b, h, i, pt: (b, h, 0, 0)),
                tr_head,
                pl.BlockSpec((1, N_META, HEAD_V), lambda b, h, i, pt: (h, 0, 0)),
                pl.BlockSpec((1, HEAD_V, N_META), lambda b, h, i, pt: (h, 0, 0)),
                tile,
                pl.BlockSpec((1, HEAD_V), lambda b, h, i, pt: (0, 0)),
                pl.BlockSpec(lams.shape, lambda b, h, i, pt: (0, 0)),
                pl.BlockSpec((1,) + dec_rows.shape[1:],
                             lambda b, h, i, pt: (dec_pos(b, h, i)[0], 0, 0, 0)),
            ] + page_specs + page_specs + [
                pl.BlockSpec(hs.shape, lambda b, h, i, pt: (0, 0)),
            ],
            out_specs=[tile, per_row],
            scratch_shapes=[acc, acc, acc, acc],
        ),
        out_shape=[jax.ShapeDtypeStruct((batch * seq, V_COLS), BF16),
                   jax.ShapeDtypeStruct((dec_batch, N_HEADS, HEAD_V), F32)],
        compiler_params=_params(("arbitrary", "arbitrary", "arbitrary"), vmem),
        name="attn_fused",
    )(page_table, qt, kb, vt, kp, vpt, zs, subln, lams,
      dec_rows, *([cache_k] * n_pg), *([cache_v] * n_pg), hs)


def _out_b_kernel(g_ref, h_ref, w_ref, y_ref):
    y_ref[...] = h_ref[...] + _dot(g_ref[...].astype(BF16), w_ref[...])


def _out_b(g, h, w_out, *, tm):
    rows = g.shape[0]
    vmem = 2 * tm * (V_COLS * 4 + 2 * D_MODEL * 4) + w_out.size * 2 + 2 * tm * D_MODEL * 4 + (8 << 20)
    return pl.pallas_call(
        _out_b_kernel,
        grid=(rows // tm,),
        in_specs=[
            pl.BlockSpec((tm, V_COLS), lambda t: (t, 0)),
            pl.BlockSpec((tm, D_MODEL), lambda t: (t, 0)),
            _resident(w_out.shape),
        ],
        out_specs=pl.BlockSpec((tm, D_MODEL), lambda t: (t, 0)),
        out_shape=jax.ShapeDtypeStruct((rows, D_MODEL), F32),
        compiler_params=_params(("arbitrary",), vmem),
        name="out_b",
    )(g, h, w_out)


def _fill_leading_rows_kernel(rows_ref, full_ref, out_ref):
    del full_ref
    out_ref[0] = rows_ref[...]


def _fill_leading_rows(full, rows):
    batch = full.shape[0]
    return pl.pallas_call(
        _fill_leading_rows_kernel,
        grid=(batch,),
        in_specs=[pl.BlockSpec(rows.shape, lambda b: (0, 0, 0)),
                  pl.BlockSpec(memory_space=pl.ANY)],
        out_specs=pl.BlockSpec((1,) + rows.shape, lambda b: (b, 0, 0, 0)),
        out_shape=jax.ShapeDtypeStruct(full.shape, full.dtype),
        input_output_aliases={1: 0},
        name="fill_leading_rows",
    )(rows, full)


def kernel(x_prompt, x_sample, state_conv, cache_k, cache_v, page_table, meta_tokens, norm_a, w_in_a, conv_w_a, w_out_a, norm_b, w_in_b, q_norm_b, k_norm_b, lambda_q1_b, lambda_k1_b, lambda_q2_b, lambda_k2_b, subln_b, w_out_b):
    batch, seq, _ = x_prompt.shape
    dec_batch, dec_seq, _ = x_sample.shape
    n_meta = meta_tokens.shape[0]
    assert norm_a.shape[0] == 1 and norm_b.shape[0] == 1 and dec_seq == 1 and n_meta == N_META
    n_pages = page_table.shape[1]
    page = cache_k.shape[2]
    past_len = n_pages * page

    w_in_a_bf = w_in_a[0].astype(BF16)
    w_out_a_bf = w_out_a[0].astype(BF16)
    w_in_b_bf = w_in_b[0].astype(BF16)
    w_out_b_bf = w_out_b[0].astype(BF16)
    gain_a = norm_a[0][None, :]
    gain_b = norm_b[0][None, :]

    x_small = jnp.concatenate([meta_tokens, x_sample[:, 0, :]], axis=0)
    zpad = jnp.zeros((n_meta, D_CONV), F32)
    p1 = jnp.concatenate([zpad, state_conv[0, :, 1, :]], axis=0)
    p2 = jnp.concatenate([zpad, state_conv[0, :, 0, :]], axis=0)
    h_small, u_small = _mixer_a_small(x_small, gain_a, w_in_a_bf, conv_w_a[0], w_out_a_bf,
                                      p1, p2, n_meta=n_meta)
    init8 = u_small[n_meta - V7X_SUBLANES:n_meta]
    h_body, st_body = _mixer_a_body(x_prompt.reshape(batch * seq, D_MODEL), gain_a, w_in_a_bf,
                                    conv_w_a[0], w_out_a_bf, init8, batch=batch, seq=seq)
    conv_prompt = st_body[:, V7X_SUBLANES - (CONV_W - 1):, :][None]
    conv_sample = jnp.stack([state_conv[0, :, 1, :], u_small[n_meta:]], axis=1)[None]

    qn_t = jnp.tile(q_norm_b[0], N_MAPS)[None, :]
    kn_t = jnp.tile(k_norm_b[0], N_MAPS)[None, :]
    gi = np.arange(V7X_MXU_DIM) // HEAD_QK
    bd = jnp.asarray(gi[:, None] == gi[None, :], dtype=BF16)
    lanes1 = jnp.ones((1, V7X_LANES), F32)
    pos_body = (n_meta + jnp.arange(seq, dtype=F32))[:, None] * lanes1
    pos_small = jnp.concatenate([jnp.arange(n_meta, dtype=F32),
                                 jnp.full((dec_batch,), float(past_len), F32)])[:, None] * lanes1
    q_s, k_small, v_small, zs_small = _proj_b(
        h_small, pos_small, gain_b, w_in_b_bf, qn_t, kn_t, bd,
        batch=1, seq=n_meta + dec_batch, tm=n_meta + dec_batch, head_major=False)
    qt_hm, k_body, v_body, zs_body, k_hm, vt_hm = _proj_b(
        h_body, pos_body, gain_b, w_in_b_bf, qn_t, kn_t, bd,
        batch=batch, seq=seq, tm=ATTN_TILE, head_major=True, row_offset=n_meta)

    lams = jnp.stack([lambda_q1_b[0], lambda_k1_b[0], lambda_q2_b[0], lambda_k2_b[0]])

    meta_heads = lambda a: a[:n_meta].reshape(n_meta, N_HEADS, HEAD_V).astype(BF16)
    per_head = lambda a: a[n_meta:].reshape(dec_batch, N_HEADS, HEAD_V)
    dec_rows = jnp.stack([per_head(q_s), per_head(k_small), per_head(v_small), per_head(zs_small)],
                         axis=1)
    li = np.arange(HEAD_V) // HEAD_QK
    half_sum = jnp.asarray(li[:, None] == li[None, :], dtype=BF16)
    g_body, g_dec = _attn_fused(
        page_table, qt_hm, k_hm, vt_hm, meta_heads(k_small).transpose(1, 0, 2),
        meta_heads(v_small).transpose(1, 2, 0), zs_body, subln_b[0][None, :], lams,
        dec_rows, cache_k, cache_v, half_sum, batch=batch, seq=seq, tq=ATTN_TILE)
    y_body = _out_b(g_body, h_body, w_out_b_bf, tm=512)
    y_dec = _out_b(g_dec.reshape(dec_batch, V_COLS), h_small[n_meta:], w_out_b_bf, tm=dec_batch)

    def with_meta(small, body):
        return _fill_leading_rows(body, small[:n_meta].reshape(n_meta, N_HEADS, HEAD_V))[None]

    return (
        y_body.reshape(batch, seq, D_MODEL),
        y_dec.reshape(dec_batch, 1, D_MODEL),
        conv_prompt,
        conv_sample,
        with_meta(k_small, k_body),
        with_meta(v_small, v_body),
        k_small[n_meta:].reshape(1, dec_batch, 1, N_HEADS, HEAD_V),
        v_small[n_meta:].reshape(1, dec_batch, 1, N_HEADS, HEAD_V),
    )
```

```python
import functools
import math

import numpy as np
import jax
import jax.numpy as jnp
from jax import lax
from jax.experimental import pallas as pl
from jax.experimental.pallas import tpu as pltpu

D_MODEL = 1024
D_CONV = 2048
CONV_W = 3
N_HEADS = 16
HEAD_QK = 64
HEAD_V = 128
ROT_DIM = 16
ROPE_THETA = 500000.0
EPS = 1e-6
N_META = 16
QK_COLS = N_HEADS * 2 * HEAD_QK
V_COLS = N_HEADS * HEAD_V
N_MAPS = 2 * N_HEADS
LAM_INIT = 0.8 - 0.6 * math.exp(-0.3 * 1)
SCORE_SCALE = HEAD_QK ** -0.5
Q_SCALE = SCORE_SCALE * math.log2(math.e)
LOG2_HEAD_QK = HEAD_QK.bit_length() - 1
LOG2_HEAD_V = HEAD_V.bit_length() - 1
LOG2_N_HEADS = N_HEADS.bit_length() - 1

V7X_LANES = 128
V7X_SUBLANES = 8
V7X_MXU_DIM = 256
V7X_VMEM_BYTES = 64 * 1024 * 1024

ATTN_TILE = 512
DECODE_CHUNK = 16

F32 = jnp.float32
BF16 = jnp.bfloat16


def _dot(a, b):
    return jnp.dot(a, b, preferred_element_type=F32)


def _rms_rows(x, gain):
    ms = jnp.mean(x * x, axis=-1, keepdims=True)
    return x * lax.rsqrt(ms + EPS) * gain


def _silu(z):
    return z * jax.nn.sigmoid(z)


def _lambda_full(lam_ref):
    a = jnp.sum(lam_ref[0:1, :] * lam_ref[1:2, :], axis=-1, keepdims=True)
    b = jnp.sum(lam_ref[2:3, :] * lam_ref[3:4, :], axis=-1, keepdims=True)
    return jnp.exp(a) - jnp.exp(b) + LAM_INIT


def _params(semantics, vmem_bytes):
    return pltpu.CompilerParams(dimension_semantics=semantics,
                                vmem_limit_bytes=min(int(vmem_bytes), V7X_VMEM_BYTES - (4 << 20)))


def _resident(shape):
    return pl.BlockSpec(shape, lambda *_: (0,) * len(shape), pipeline_mode=pl.Buffered(1))


def _conv_gate_out(xn, u, um1, um2, wb, wz, cw, wout):
    conv = cw[0:1, :] * um2 + cw[1:2, :] * um1 + cw[2:3, :] * u
    y = _dot(xn, wb) * conv * _silu(_dot(xn, wz))
    return _dot(y.astype(BF16), wout)


def _mixer_a_body_kernel(x_ref, g_ref, win_ref, cw_ref, wout_ref, init_ref,
                         hp_ref, st_ref, carry_ref, *, tm, tc):
    @pl.when(pl.program_id(1) == 0)
    def _():
        carry_ref[...] = init_ref[...]

    x = x_ref[...]
    xn = _rms_rows(x, g_ref[...]).astype(BF16)
    row = lax.broadcasted_iota(jnp.int32, (tm, tc), 0)
    acc = jnp.zeros((tm, D_MODEL), F32)
    for j in range(D_CONV // tc):
        lo = j * tc
        u = _dot(xn, win_ref[:, lo:lo + tc]) * _dot(xn, win_ref[:, 2 * D_CONV + lo:2 * D_CONV + lo + tc])
        prev = carry_ref[:, lo:lo + tc]
        p1 = prev[7:8, :]
        p2 = prev[6:7, :]
        um1 = jnp.where(row == 0, p1, pltpu.roll(u, 1, 0))
        um2 = jnp.where(row == 0, p2, jnp.where(row == 1, p1, pltpu.roll(u, 2, 0)))
        carry_ref[:, lo:lo + tc] = u[tm - V7X_SUBLANES:tm, :]
        acc = acc + _conv_gate_out(
            xn, u, um1, um2,
            win_ref[:, D_CONV + lo:D_CONV + lo + tc],
            win_ref[:, 3 * D_CONV + lo:3 * D_CONV + lo + tc],
            cw_ref[:, lo:lo + tc], wout_ref[lo:lo + tc, :])
    hp_ref[...] = x + acc
    st_ref[0] = carry_ref[...]


def _mixer_a_small_kernel(x_ref, g_ref, wc_ref, wb_ref, wv_ref, wz_ref, cw_ref, wout_ref,
                          p1_ref, p2_ref, hp_ref, u_ref, *, n_meta):
    x = x_ref[...]
    xn = _rms_rows(x, g_ref[...]).astype(BF16)
    u = _dot(xn, wc_ref[...]) * _dot(xn, wv_ref[...])
    row = lax.broadcasted_iota(jnp.int32, u.shape, 0)
    um1 = jnp.where(row == 0, 0.0, jnp.where(row < n_meta, pltpu.roll(u, 1, 0), p1_ref[...]))
    um2 = jnp.where(row < 2, 0.0, jnp.where(row < n_meta, pltpu.roll(u, 2, 0), p2_ref[...]))
    u_ref[...] = u
    contrib = _conv_gate_out(xn, u, um1, um2, wb_ref[...], wz_ref[...], cw_ref[...], wout_ref[...])

    @pl.when(pl.program_id(0) == 0)
    def _():
        hp_ref[...] = x + contrib

    @pl.when(pl.program_id(0) != 0)
    def _():
        hp_ref[...] += contrib


def _mixer_a_body(x2d, gain, w_in, conv_w, w_out, init8, *, batch, seq, tm=1024, tc=256):
    nt = seq // tm
    vmem = (4 * tm * D_MODEL * 4
            + w_in.size * 2 + w_out.size * 2
            + 10 * tm * tc * 4 + 2 * tm * D_MODEL * 4
            + (8 << 20))
    return pl.pallas_call(
        functools.partial(_mixer_a_body_kernel, tm=tm, tc=tc),
        grid=(batch, nt),
        in_specs=[
            pl.BlockSpec((tm, D_MODEL), lambda b, t: (b * nt + t, 0)),
            _resident((1, D_MODEL)),
            _resident(w_in.shape),
            _resident(conv_w.shape),
            _resident(w_out.shape),
            _resident(init8.shape),
        ],
        out_specs=[
            pl.BlockSpec((tm, D_MODEL), lambda b, t: (b * nt + t, 0)),
            pl.BlockSpec((1, V7X_SUBLANES, D_CONV), lambda b, t: (b, 0, 0)),
        ],
        out_shape=[
            jax.ShapeDtypeStruct((batch * seq, D_MODEL), F32),
            jax.ShapeDtypeStruct((batch, V7X_SUBLANES, D_CONV), F32),
        ],
        scratch_shapes=[pltpu.VMEM((V7X_SUBLANES, D_CONV), F32)],
        compiler_params=_params(("arbitrary", "arbitrary"), vmem),
        name="mixer_a_body",
    )(x2d, gain, w_in, conv_w, w_out, init8)


def _mixer_a_small(x, gain, w_in, conv_w, w_out, p1, p2, *, n_meta, tc=512):
    rows = x.shape[0]
    nj = D_CONV // tc
    w_spec = lambda k: pl.BlockSpec((D_MODEL, tc), lambda j, k=k: (0, k * nj + j))
    vmem = 2 * (4 * D_MODEL * tc * 2 + tc * D_MODEL * 2) + (8 << 20)
    return pl.pallas_call(
        functools.partial(_mixer_a_small_kernel, n_meta=n_meta),
        grid=(nj,),
        in_specs=[
            pl.BlockSpec((rows, D_MODEL), lambda j: (0, 0)),
            pl.BlockSpec((1, D_MODEL), lambda j: (0, 0)),
            w_spec(0), w_spec(1), w_spec(2), w_spec(3),
            pl.BlockSpec((CONV_W, tc), lambda j: (0, j)),
            pl.BlockSpec((tc, D_MODEL), lambda j: (j, 0)),
            pl.BlockSpec((rows, tc), lambda j: (0, j)),
            pl.BlockSpec((rows, tc), lambda j: (0, j)),
        ],
        out_specs=[
            pl.BlockSpec((rows, D_MODEL), lambda j: (0, 0)),
            pl.BlockSpec((rows, tc), lambda j: (0, j)),
        ],
        out_shape=[
            jax.ShapeDtypeStruct((rows, D_MODEL), F32),
            jax.ShapeDtypeStruct((rows, D_CONV), F32),
        ],
        compiler_params=_params(("arbitrary",), vmem),
        name="mixer_a_small",
    )(x, gain, w_in, w_in, w_in, w_in, conv_w, w_out, p1, p2)


def _rope_tables(pos, tc):
    lane = lax.broadcasted_iota(jnp.int32, (1, V7X_LANES), 1)
    l64 = lane & (HEAD_QK - 1)
    half = ROT_DIM // 2
    idx = jnp.where(l64 < half, l64, l64 - half).astype(F32)
    inv = jnp.where(l64 < ROT_DIM, jnp.exp(idx * (-2.0 / ROT_DIM * math.log(ROPE_THETA))), 0.0)
    ang = pos * inv
    cos = jnp.cos(ang)
    sin = jnp.sin(ang)
    lo = jnp.where(l64 < half, -sin, 0.0)
    hi = jnp.where((l64 >= half) & (l64 < ROT_DIM), sin, 0.0)
    reps = tc // V7X_LANES
    return tuple(jnp.concatenate([t] * reps, axis=1) for t in (cos, lo, hi))


def _norm_rope(p, gain, bd, tables):
    tc = p.shape[1]
    sq = (p * p).astype(BF16)
    ss = jnp.concatenate(
        [_dot(sq[:, i * V7X_MXU_DIM:(i + 1) * V7X_MXU_DIM], bd) for i in range(tc // V7X_MXU_DIM)],
        axis=1)
    pn = p * lax.rsqrt(ss * (1.0 / HEAD_QK) + EPS) * gain
    cos, lo, hi = tables
    half = ROT_DIM // 2
    return pn * cos + pltpu.roll(pn, tc - half, 1) * lo + pltpu.roll(pn, half, 1) * hi


def _proj_b_kernel(h_ref, pos_ref, g_ref, win_ref, qn_ref, kn_ref, bd_ref, *out_refs,
                   tc, head_major):
    xn = _rms_rows(h_ref[...], g_ref[...]).astype(BF16)
    tables = _rope_tables(pos_ref[...], tc)
    bd = bd_ref[...]
    heads_per_chunk = tc // HEAD_V
    if head_major:
        _, _, q_ref, kf_ref, vf_ref, zs_ref, kb_ref, vt_ref = out_refs
    else:
        q_ref, kf_ref, vf_ref, zs_ref = out_refs
    for j in range(QK_COLS // tc):
        lo = j * tc
        q = _norm_rope(_dot(xn, win_ref[:, lo:lo + tc]), qn_ref[:, lo:lo + tc], bd, tables)
        q = q * Q_SCALE
        k = _norm_rope(_dot(xn, win_ref[:, QK_COLS + lo:QK_COLS + lo + tc]),
                       kn_ref[:, lo:lo + tc], bd, tables)
        v = _dot(xn, win_ref[:, 2 * QK_COLS + lo:2 * QK_COLS + lo + tc])
        z = _dot(xn, win_ref[:, 2 * QK_COLS + V_COLS + lo:2 * QK_COLS + V_COLS + lo + tc])
        zs_ref[:, lo:lo + tc] = _silu(z).astype(zs_ref.dtype)
        if head_major:
            hlo = j * heads_per_chunk
            kf_ref[0, :, hlo:hlo + heads_per_chunk, :] = k.reshape(k.shape[0], heads_per_chunk, HEAD_V)
            vf_ref[0, :, hlo:hlo + heads_per_chunk, :] = v.reshape(v.shape[0], heads_per_chunk, HEAD_V)
            for hh in range(heads_per_chunk):
                h = j * heads_per_chunk + hh
                sl = slice(hh * HEAD_V, (hh + 1) * HEAD_V)
                q_ref[0, h, 0] = q[:, sl].T.astype(BF16)
                kb_ref[0, h] = k[:, sl].astype(BF16)
                vt_ref[0, h, 0] = v[:, sl].T.astype(BF16)
        else:
            kf_ref[:, lo:lo + tc] = k
            vf_ref[:, lo:lo + tc] = v
            q_ref[:, lo:lo + tc] = q


def _proj_b(h2d, pos, gain, w_in, qn, kn, bd, *, batch, seq, tm, head_major, kv_base=None,
            row_offset=0, tc=512):
    nt = seq // tm
    rows = batch * seq
    row_spec = lambda cols: pl.BlockSpec((tm, cols), lambda b, t: (b * nt + t, 0))
    hm_spec = pl.BlockSpec((1, N_HEADS, tm, HEAD_V), lambda b, t: (b, 0, t, 0))
    hm_shape = jax.ShapeDtypeStruct((batch, N_HEADS, seq, HEAD_V), BF16)
    tr_spec = pl.BlockSpec((1, N_HEADS, 1, HEAD_V, tm), lambda b, t: (b, 0, t, 0, 0))
    tr_shape = jax.ShapeDtypeStruct((batch, N_HEADS, nt, HEAD_V, tm), BF16)
    row_shape = lambda dt: jax.ShapeDtypeStruct((rows, QK_COLS), dt)
    if head_major:
        fin_spec = pl.BlockSpec(
            (pl.Element(1), pl.Element(tm), pl.Element(N_HEADS), pl.Element(HEAD_V)),
            lambda b, t: (b, row_offset + t * tm, 0, 0))
        fin_shape = jax.ShapeDtypeStruct((batch, row_offset + seq, N_HEADS, HEAD_V), F32)
        out_specs = [tr_spec, fin_spec, fin_spec, row_spec(V_COLS), hm_spec, tr_spec]
        out_shape = [tr_shape, fin_shape, fin_shape, row_shape(BF16), hm_shape, tr_shape]
    else:
        out_specs = [row_spec(QK_COLS), row_spec(QK_COLS), row_spec(V_COLS), row_spec(V_COLS)]
        out_shape = [row_shape(F32), row_shape(F32), row_shape(F32), row_shape(F32)]
    vmem = (2 * tm * D_MODEL * 4 + w_in.size * 2
            + 2 * tm * QK_COLS * (4 + 4 + 4 + 4 + 4)
            + 12 * tm * tc * 4 + (8 << 20))
    in_specs = [
        row_spec(D_MODEL),
        pl.BlockSpec((tm, V7X_LANES), lambda b, t: (t, 0)),
        _resident((1, D_MODEL)),
        _resident(w_in.shape),
        _resident((1, QK_COLS)),
        _resident((1, QK_COLS)),
        _resident(bd.shape),
    ]
    operands = [h2d, pos, gain, w_in, qn, kn, bd]
    aliases = {}
    if head_major:
        aliases = {len(operands): 1, len(operands) + 1: 2}
        in_specs += [pl.BlockSpec(memory_space=pl.ANY)] * 2
        operands += list(kv_base)
    return pl.pallas_call(
        functools.partial(_proj_b_kernel, tc=tc, head_major=head_major),
        grid=(batch, nt),
        in_specs=in_specs,
        out_specs=out_specs,
        out_shape=out_shape,
        input_output_aliases=aliases,
        compiler_params=_params(("arbitrary", "arbitrary"), vmem),
        name="proj_b_body" if head_major else "proj_b_small",
    )(*operands)


def _online_update(state, st, vt):
    m, l, acc = state
    m_new = jnp.maximum(m, jnp.max(st, axis=0, keepdims=True))
    alpha = jnp.exp2(m - m_new)
    pt = jnp.exp2(st - m_new)
    return (m_new, alpha * l + jnp.sum(pt, axis=0, keepdims=True),
            alpha * acc + _dot(vt, pt.astype(BF16)))


def _attn_full_blocks(i, qt, k_ref, vt_ref, *, tq):
    drow = lax.broadcasted_iota(jnp.int32, qt.shape, 0)
    zero = jnp.zeros_like(qt)
    qtm = (jnp.where(drow < HEAD_QK, qt, zero), jnp.where(drow >= HEAD_QK, qt, zero))

    def scores(j):
        kb = k_ref[0, 0, pl.ds(pl.multiple_of(j * tq, tq), tq), :]
        return tuple(_dot(kb, qtm[c]) for c in range(2))

    empty = (jnp.full((1, tq), -jnp.inf, F32), jnp.zeros((1, tq), F32), jnp.zeros((HEAD_V, tq), F32))
    n_odd = i & 1

    def one_block(j, states):
        sts = scores(j)
        vt = vt_ref[0, 0, j]
        return tuple(_online_update(states[c], sts[c], vt) for c in range(2))

    def two_blocks(p, states):
        ja = n_odd + 2 * p
        sa = scores(ja)
        sb = scores(ja + 1)
        va = vt_ref[0, 0, ja]
        vb = vt_ref[0, 0, ja + 1]
        states = tuple(_online_update(states[c], sa[c], va) for c in range(2))
        return tuple(_online_update(states[c], sb[c], vb) for c in range(2))

    states = lax.fori_loop(0, n_odd, one_block, (empty, empty))
    states = lax.fori_loop(0, lax.shift_right_logical(i, 1), two_blocks, states)
    return states, qtm


def _attn_last_blocks(i, states, qtm, k_ref, vt_ref, kp, vpt, lam, *, tq):
    causal = (lax.broadcasted_iota(jnp.int32, (tq, tq), 0)
              <= lax.broadcasted_iota(jnp.int32, (tq, tq), 1))
    kd = k_ref[0, 0, pl.ds(pl.multiple_of(i * tq, tq), tq), :]
    sts = tuple(_dot(kd, qtm[c]) for c in range(2))
    sps = tuple(_dot(kp, qtm[c]) for c in range(2))
    vt = vt_ref[0, 0, i]
    outs = []
    for c in range(2):
        m, l, acc = states[c]
        sd = jnp.where(causal, sts[c], -jnp.inf)
        m_new = jnp.maximum(m, jnp.maximum(jnp.max(sd, axis=0, keepdims=True),
                                           jnp.max(sps[c], axis=0, keepdims=True)))
        alpha = jnp.exp2(m - m_new)
        pd = jnp.exp2(sd - m_new)
        pp = jnp.exp2(sps[c] - m_new)
        l = alpha * l + jnp.sum(pd, axis=0, keepdims=True) + jnp.sum(pp, axis=0, keepdims=True)
        acc = alpha * acc + _dot(vt, pd.astype(BF16)) + _dot(vpt, pp.astype(BF16))
        outs.append(acc * (1.0 / l))
    return (outs[0] - lam * outs[1]).T


def _half_sums(x):
    lane = lax.broadcasted_iota(jnp.int32, x.shape, 1)
    low = lane < HEAD_QK
    s_lo = jnp.sum(jnp.where(low, x, 0.0), axis=1, keepdims=True)
    s_hi = jnp.sum(jnp.where(low, 0.0, x), axis=1, keepdims=True)
    return jnp.where(low, s_lo, s_hi)


def _decode_reset(m_ref, l_ref, acc_ref, accx_ref):
    m_ref[...] = jnp.full(m_ref.shape, -jnp.inf, F32)
    l_ref[...] = jnp.zeros(l_ref.shape, F32)
    acc_ref[...] = jnp.zeros(acc_ref.shape, F32)
    accx_ref[...] = jnp.zeros(accx_ref.shape, F32)


def _decode_pages(q, hs, k_refs, v_refs, m_ref, l_ref, acc_ref, accx_ref, *, page):
    n_pg = len(k_refs)

    def page_scores(k_ref):
        prod = (k_ref[0, 0] * q[None]).astype(BF16).reshape(page * N_HEADS, HEAD_V)
        return _dot(prod, hs).reshape(page, N_HEADS, HEAD_V)

    m = m_ref[...]
    l = l_ref[...]
    acc = acc_ref[...]
    accx = accx_ref[...]
    s_cur = page_scores(k_refs[0])
    for i in range(n_pg):
        s_next = page_scores(k_refs[i + 1]) if i + 1 < n_pg else None
        m_new = jnp.maximum(m, jnp.max(s_cur, axis=0))
        alpha = jnp.exp2(m - m_new)
        l = alpha * l
        acc = alpha * acc
        accx = pltpu.roll(alpha, HEAD_QK, 1) * accx
        for t0 in range(0, page, DECODE_CHUNK):
            p3 = jnp.exp2(s_cur[t0:t0 + DECODE_CHUNK] - m_new[None])
            px3 = pltpu.roll(p3.reshape(DECODE_CHUNK * N_HEADS, HEAD_V), HEAD_QK, 1).reshape(p3.shape)
            v3 = v_refs[i][0, 0, t0:t0 + DECODE_CHUNK]
            l = l + jnp.sum(p3, axis=0)
            acc = acc + jnp.sum(v3 * p3, axis=0)
            accx = accx + jnp.sum(v3 * px3, axis=0)
        m = m_new
        s_cur = s_next
    m_ref[...] = m
    l_ref[...] = l
    acc_ref[...] = acc
    accx_ref[...] = accx


def _decode_finish(q, kn, vn, zs, sub, lam, m_ref, l_ref, acc_ref, accx_ref):
    s_self = _half_sums(q * kn)
    m_old = m_ref[...]
    m_fin = jnp.maximum(m_old, s_self)
    a_fin = jnp.exp2(m_old - m_fin)
    p_self = jnp.exp2(s_self - m_fin)
    inv_l = 1.0 / (a_fin * l_ref[...] + p_self)
    w = (a_fin * acc_ref[...] + p_self * vn) * inv_l
    a_x = pltpu.roll(a_fin, HEAD_QK, 1)
    p_x = pltpu.roll(p_self, HEAD_QK, 1)
    wx = (a_x * accx_ref[...] + p_x * vn) * pltpu.roll(inv_l, HEAD_QK, 1)
    low = lax.broadcasted_iota(jnp.int32, w.shape, 1) < HEAD_QK
    d = jnp.where(low, w, wx) - lam * jnp.where(low, wx, w)
    return _rms_rows(d, sub) * (1.0 - LAM_INIT) * zs


def _attn_fused_kernel(pt_ref, qt_ref, k_ref, vt_ref, kp_ref, vpt_ref, zs_ref, sub_ref,
                       lam_ref, dec_ref, *refs, tq, page, n_pg, groups):
    del pt_ref
    k_pages = refs[:n_pg]
    v_pages = refs[n_pg:2 * n_pg]
    hs_ref, g_ref, gd_ref, m_ref, l_ref, acc_ref, accx_ref = refs[2 * n_pg:]
    i = pl.program_id(2)
    step = (pl.program_id(0) * pl.num_programs(1) + pl.program_id(1)) * pl.num_programs(2) + i
    grp = step & (groups - 1)
    dec_state = (m_ref, l_ref, acc_ref, accx_ref)

    @pl.when(grp == 0)
    def _():
        _decode_reset(*dec_state)

    qt = qt_ref[0, 0, 0]
    states, qtm = _attn_full_blocks(i, qt, k_ref, vt_ref, tq=tq)

    qd = dec_ref[0, 0]
    half = n_pg // 2
    _decode_pages(qd, hs_ref[...], k_pages[:half], v_pages[:half], *dec_state, page=page)
    lam = _lambda_full(lam_ref)
    o = _attn_last_blocks(i, states, qtm, k_ref, vt_ref, kp_ref[0], vpt_ref[0], lam, tq=tq)
    _decode_pages(qd, hs_ref[...], k_pages[half:], v_pages[half:], *dec_state, page=page)
    on = _rms_rows(o, sub_ref[...]) * (1.0 - LAM_INIT)
    g_ref[...] = (on * zs_ref[...].astype(F32)).astype(g_ref.dtype)

    @pl.when(grp == groups - 1)
    def _():
        gd_ref[0] = _decode_finish(qd, dec_ref[0, 1], dec_ref[0, 2], dec_ref[0, 3], sub_ref[...],
                                   lam, *dec_state)


def _attn_fused(page_table, qt, kb, vt, kp, vpt, zs, subln, lams,
                dec_rows, cache_k, cache_v, hs, *, batch, seq, tq, n_pg=4):
    nq = seq // tq
    dec_batch, n_pages = page_table.shape
    page = cache_k.shape[2]
    groups = n_pages // n_pg
    assert n_pages % n_pg == 0 and batch * N_HEADS * nq == dec_batch * groups
    assert groups & (groups - 1) == 0

    def dec_pos(b, h, i):
        step = (b * N_HEADS + h) * nq + i
        return lax.shift_right_logical(step, groups.bit_length() - 1), step & (groups - 1)

    per_row = pl.BlockSpec((1, N_HEADS, HEAD_V), lambda b, h, i, pt: (dec_pos(b, h, i)[0], 0, 0))

    def page_spec(j):
        def index(b, h, i, pt):
            row, grp = dec_pos(b, h, i)
            return (0, pt[row, grp * n_pg + j], 0, 0, 0)
        return pl.BlockSpec((1, 1, page, N_HEADS, HEAD_V), index)

    page_specs = [page_spec(j) for j in range(n_pg)]
    tr_head = pl.BlockSpec((1, 1, nq, HEAD_V, tq), lambda b, h, i, pt: (b, h, 0, 0, 0))
    tile = pl.BlockSpec((tq, HEAD_V), lambda b, h, i, pt: (b * nq + i, h))
    acc = pltpu.VMEM((N_HEADS, HEAD_V), F32)
    page_bytes = page * N_HEADS * HEAD_V * 4
    vmem = (n_pg * (4 * page_bytes + 5 * page_bytes)
            + 4 * seq * HEAD_V * 2 + 10 * tq * tq * 4 + (8 << 20))
    return pl.pallas_call(
        functools.partial(_attn_fused_kernel, tq=tq, page=page, n_pg=n_pg, groups=groups),
        grid_spec=pltpu.PrefetchScalarGridSpec(
            num_scalar_prefetch=1,
            grid=(batch, N_HEADS, nq),
            in_specs=[
                pl.BlockSpec((1, 1, 1, HEAD_V, tq), lambda b, h, i, pt: (b, h, i, 0, 0)),
                pl.BlockSpec((1, 1, seq, HEAD_V), lambda b, h, i, pt: (b, h, 0, 0)),
                tr_head,
                pl.BlockSpec((1, N_META, HEAD_V), lambda b, h, i, pt: (h, 0, 0)),
                pl.BlockSpec((1, HEAD_V, N_META), lambda b, h, i, pt: (h, 0, 0)),
                tile,
                pl.BlockSpec((1, HEAD_V), lambda b, h, i, pt: (0, 0)),
                pl.BlockSpec(lams.shape, lambda b, h, i, pt: (0, 0)),
                pl.BlockSpec((1,) + dec_rows.shape[1:],
                             lambda b, h, i, pt: (dec_pos(b, h, i)[0], 0, 0, 0)),
            ] + page_specs + page_specs + [
                pl.BlockSpec(hs.shape, lambda b, h, i, pt: (0, 0)),
            ],
            out_specs=[tile, per_row],
            scratch_shapes=[acc, acc, acc, acc],
        ),
        out_shape=[jax.ShapeDtypeStruct((batch * seq, V_COLS), BF16),
                   jax.ShapeDtypeStruct((dec_batch, N_HEADS, HEAD_V), F32)],
        compiler_params=_params(("arbitrary", "arbitrary", "arbitrary"), vmem),
        name="attn_fused",
    )(page_table, qt, kb, vt, kp, vpt, zs, subln, lams,
      dec_rows, *([cache_k] * n_pg), *([cache_v] * n_pg), hs)


def _out_b_kernel(g_ref, h_ref, w_ref, y_ref):
    y_ref[...] = h_ref[...] + _dot(g_ref[...].astype(BF16), w_ref[...])


def _out_b(g, h, w_out, *, tm):
    rows = g.shape[0]
    vmem = 2 * tm * (V_COLS * 4 + 2 * D_MODEL * 4) + w_out.size * 2 + 2 * tm * D_MODEL * 4 + (8 << 20)
    return pl.pallas_call(
        _out_b_kernel,
        grid=(rows // tm,),
        in_specs=[
            pl.BlockSpec((tm, V_COLS), lambda t: (t, 0)),
            pl.BlockSpec((tm, D_MODEL), lambda t: (t, 0)),
            _resident(w_out.shape),
        ],
        out_specs=pl.BlockSpec((tm, D_MODEL), lambda t: (t, 0)),
        out_shape=jax.ShapeDtypeStruct((rows, D_MODEL), F32),
        compiler_params=_params(("arbitrary",), vmem),
        name="out_b",
    )(g, h, w_out)


def kernel(x_prompt, x_sample, state_conv, cache_k, cache_v, page_table, meta_tokens, norm_a, w_in_a, conv_w_a, w_out_a, norm_b, w_in_b, q_norm_b, k_norm_b, lambda_q1_b, lambda_k1_b, lambda_q2_b, lambda_k2_b, subln_b, w_out_b):
    batch, seq, _ = x_prompt.shape
    dec_batch, dec_seq, _ = x_sample.shape
    n_meta = meta_tokens.shape[0]
    assert norm_a.shape[0] == 1 and norm_b.shape[0] == 1 and dec_seq == 1 and n_meta == N_META
    n_pages = page_table.shape[1]
    page = cache_k.shape[2]
    past_len = n_pages * page

    w_in_a_bf = w_in_a[0].astype(BF16)
    w_out_a_bf = w_out_a[0].astype(BF16)
    w_in_b_bf = w_in_b[0].astype(BF16)
    w_out_b_bf = w_out_b[0].astype(BF16)
    gain_a = norm_a[0][None, :]
    gain_b = norm_b[0][None, :]

    x_small = jnp.concatenate([meta_tokens, x_sample[:, 0, :]], axis=0)
    zpad = jnp.zeros((n_meta, D_CONV), F32)
    p1 = jnp.concatenate([zpad, state_conv[0, :, 1, :]], axis=0)
    p2 = jnp.concatenate([zpad, state_conv[0, :, 0, :]], axis=0)
    h_small, u_small = _mixer_a_small(x_small, gain_a, w_in_a_bf, conv_w_a[0], w_out_a_bf,
                                      p1, p2, n_meta=n_meta)
    init8 = u_small[n_meta - V7X_SUBLANES:n_meta]
    h_body, st_body = _mixer_a_body(x_prompt.reshape(batch * seq, D_MODEL), gain_a, w_in_a_bf,
                                    conv_w_a[0], w_out_a_bf, init8, batch=batch, seq=seq)
    conv_prompt = st_body[:, V7X_SUBLANES - (CONV_W - 1):, :][None]
    conv_sample = jnp.stack([state_conv[0, :, 1, :], u_small[n_meta:]], axis=1)[None]

    qn_t = jnp.tile(q_norm_b[0], N_MAPS)[None, :]
    kn_t = jnp.tile(k_norm_b[0], N_MAPS)[None, :]
    gi = np.arange(V7X_MXU_DIM) // HEAD_QK
    bd = jnp.asarray(gi[:, None] == gi[None, :], dtype=BF16)
    lanes1 = jnp.ones((1, V7X_LANES), F32)
    pos_body = (n_meta + jnp.arange(seq, dtype=F32))[:, None] * lanes1
    pos_small = jnp.concatenate([jnp.arange(n_meta, dtype=F32),
                                 jnp.full((dec_batch,), float(past_len), F32)])[:, None] * lanes1
    q_s, k_small, v_small, zs_small = _proj_b(
        h_small, pos_small, gain_b, w_in_b_bf, qn_t, kn_t, bd,
        batch=1, seq=n_meta + dec_batch, tm=n_meta + dec_batch, head_major=False)
    def meta_then_zeros(small):
        meta = jnp.broadcast_to(small[:n_meta].reshape(1, n_meta, N_HEADS, HEAD_V),
                                (batch, n_meta, N_HEADS, HEAD_V))
        return jnp.pad(meta, ((0, 0), (0, seq), (0, 0), (0, 0)))

    qt_hm, k_prompt, v_prompt, zs_body, k_hm, vt_hm = _proj_b(
        h_body, pos_body, gain_b, w_in_b_bf, qn_t, kn_t, bd,
        batch=batch, seq=seq, tm=ATTN_TILE, head_major=True, row_offset=n_meta,
        kv_base=(meta_then_zeros(k_small), meta_then_zeros(v_small)))

    lams = jnp.stack([lambda_q1_b[0], lambda_k1_b[0], lambda_q2_b[0], lambda_k2_b[0]])

    meta_heads = lambda a: a[:n_meta].reshape(n_meta, N_HEADS, HEAD_V).astype(BF16)
    per_head = lambda a: a[n_meta:].reshape(dec_batch, N_HEADS, HEAD_V)
    dec_rows = jnp.stack([per_head(q_s), per_head(k_small), per_head(v_small), per_head(zs_small)],
                         axis=1)
    li = np.arange(HEAD_V) // HEAD_QK
    half_sum = jnp.asarray(li[:, None] == li[None, :], dtype=BF16)
    g_body, g_dec = _attn_fused(
        page_table, qt_hm, k_hm, vt_hm, meta_heads(k_small).transpose(1, 0, 2),
        meta_heads(v_small).transpose(1, 2, 0), zs_body, subln_b[0][None, :], lams,
        dec_rows, cache_k, cache_v, half_sum, batch=batch, seq=seq, tq=ATTN_TILE)
    y_body = _out_b(g_body, h_body, w_out_b_bf, tm=512)
    y_dec = _out_b(g_dec.reshape(dec_batch, V_COLS), h_small[n_meta:], w_out_b_bf, tm=dec_batch)

    return (
        y_body.reshape(batch, seq, D_MODEL),
        y_dec.reshape(dec_batch, 1, D_MODEL),
        conv_prompt,
        conv_sample,
        k_prompt[None],
        v_prompt[None],
        k_small[n_meta:].reshape(1, dec_batch, 1, N_HEADS, HEAD_V),
        v_small[n_meta:].reshape(1, dec_batch, 1, N_HEADS, HEAD_V),
    )
```
